```python
import math
import jax, jax.numpy as jnp
from jax import lax
import numpy as np

D_MODEL = 1024
BATCH = 8
SEQ = 8192
DEPTH = 1
DEC_BATCH = 128
DEC_SEQ = 4
PAST_LEN = 8192
PAGE_SIZE = 128

H_A = 4
HD_A = 64
W_A = H_A * 2 * HD_A
H_B = 8
HD_B = 64
W_B = H_B * HD_B
MOBA_BLOCK = 256
MOBA_TOPK = 3
ROPE_THETA = 500000.0
ROPE_FRACTION = 4
N_GROUPS = 4
EXPERTS_PER_GROUP = 8
N_EXPERTS = N_GROUPS * EXPERTS_PER_GROUP
TOPK_IN_GROUP = 2
D_EXPERT = 256
LN_EPS = 1e-5
ALPHA = (2.0 * DEPTH) ** 0.25
BETA = (8.0 * DEPTH) ** -0.25
DIFF_Q_BLOCK = 128
MOBA_Q_CHUNK = 64
NEG = -1e30
SCALE_A = HD_A ** -0.5
SCALE_B = HD_B ** -0.5
W_IN_SPLITS = (W_A, 2 * W_A, 3 * W_A, 3 * W_A + W_B, 3 * W_A + 2 * W_B, 3 * W_A + 3 * W_B,
               3 * W_A + 3 * W_B + D_MODEL)
W_IN_COLS = 3 * W_A + 3 * W_B + 2 * D_MODEL

kernel_name = "diffattn_moba_hiermoe_decoder_step"


def _layer_norm(x, g, b):
    xf = x.astype(jnp.float32)
    mu = jnp.mean(xf, axis=-1, keepdims=True)
    xc = xf - mu
    var = jnp.mean(xc * xc, axis=-1, keepdims=True)
    return (xc * lax.rsqrt(var + LN_EPS) * g.astype(jnp.float32) + b.astype(jnp.float32)).astype(x.dtype)


def _rope(x, pos):
    rot = x.shape[-1] // ROPE_FRACTION
    half = rot // 2
    inv_freq = ROPE_THETA ** (-jnp.arange(half, dtype=jnp.float32) * 2.0 / rot)
    ang = pos.astype(jnp.float32)[:, None] * inv_freq[None, :]
    cos = jnp.cos(ang)[:, None, :]
    sin = jnp.sin(ang)[:, None, :]
    xf = x.astype(jnp.float32)
    x1, x2, rest = xf[..., :half], xf[..., half:rot], xf[..., rot:]
    return jnp.concatenate([x1 * cos - x2 * sin, x2 * cos + x1 * sin, rest], axis=-1).astype(x.dtype)


def _attn_inputs(x, pos, w_in):
    bsz, L, _ = x.shape
    p = jnp.einsum('bld,de->ble', x, w_in)
    q_a, k_a, v_a, q_b, k_b, v_b, g_a, g_b = jnp.split(p, W_IN_SPLITS, axis=-1)
    q_a = _rope(q_a.reshape(bsz, L, 2 * H_A, HD_A), pos)
    k_a = _rope(k_a.reshape(bsz, L, 2 * H_A, HD_A), pos)
    v_a = v_a.reshape(bsz, L, H_A, 2 * HD_A)
    q_b = _rope(q_b.reshape(bsz, L, H_B, HD_B), pos)
    k_b = _rope(k_b.reshape(bsz, L, H_B, HD_B), pos)
    v_b = v_b.reshape(bsz, L, H_B, HD_B)
    return q_a, k_a, v_a, q_b, k_b, v_b, jax.nn.sigmoid(g_a), jax.nn.sigmoid(g_b)


def _over_query_chunks(fn, q, q_pos, chunk):
    n = q.shape[0] // chunk
    qs = q.reshape((n, chunk) + q.shape[1:])
    ps = q_pos.reshape(n, chunk)
    out = lax.map(lambda a: fn(a[0], a[1]), (qs, ps))
    return out.reshape((n * chunk,) + out.shape[2:])


def _diff_chunk(q, q_pos, k, v, lam, subln_g, lambda_init):
    lq, lk = q.shape[0], k.shape[0]
    s = jnp.einsum('qhd,khd->hqk', q.astype(jnp.float32), k.astype(jnp.float32)) * SCALE_A
    mask = jnp.arange(lk)[None, :] <= q_pos[:, None]
    s = jnp.where(mask[None], s, NEG)
    p = jax.nn.softmax(s, axis=-1).reshape(H_A, 2, lq, lk)
    a = p[:, 0] - lam * p[:, 1]
    o = jnp.einsum('hqk,khe->qhe', a, v.astype(jnp.float32))
    o = o * lax.rsqrt(jnp.mean(o * o, axis=-1, keepdims=True) + LN_EPS)
    o = o * subln_g.astype(jnp.float32) * (1.0 - lambda_init)
    return o.reshape(lq, W_A)


def _moba_blocks(k, v):
    L = k.shape[0]
    nb = -(-L // MOBA_BLOCK)
    pad = nb * MOBA_BLOCK - L
    kp = jnp.pad(k, ((0, pad), (0, 0), (0, 0))).reshape(nb, MOBA_BLOCK, H_B, HD_B).transpose(2, 0, 1, 3)
    vp = jnp.pad(v, ((0, pad), (0, 0), (0, 0))).reshape(nb, MOBA_BLOCK, H_B, HD_B).transpose(2, 0, 1, 3)
    means = jnp.mean(kp.astype(jnp.float32), axis=2)
    return kp, vp, means


def _moba_chunk(q, q_pos, kb, vb, means):
    lq = q.shape[0]
    nb = kb.shape[1]
    k_eff = min(MOBA_TOPK, nb)
    qf = q.astype(jnp.float32)
    own = q_pos // MOBA_BLOCK
    gate = jnp.einsum('qhd,hnd->qhn', qf, means)
    past = jnp.arange(nb)[None, None, :] < own[:, None, None]
    gate = jnp.where(past, gate, NEG)
    _, top = lax.top_k(gate, k_eff)
    own_b = jnp.broadcast_to(own[:, None, None], (lq, H_B, 1))
    sel = jnp.concatenate([top, own_b], axis=-1)
    valid = jnp.concatenate([top < own[:, None, None], jnp.ones((lq, H_B, 1), dtype=bool)], axis=-1)
    h_idx = jnp.arange(H_B)[None, :, None]
    ks = kb[h_idx, sel]
    vs = vb[h_idx, sel]
    kpos = sel[..., None] * MOBA_BLOCK + jnp.arange(MOBA_BLOCK)
    mask = valid[..., None] & (kpos <= q_pos[:, None, None, None])
    s = jnp.einsum('qhd,qhnkd->qhnk', qf, ks.astype(jnp.float32)) * SCALE_B
    s = jnp.where(mask, s, NEG).reshape(lq, H_B, -1)
    p = jax.nn.softmax(s, axis=-1)
    o = jnp.einsum('qhm,qhmd->qhd', p, vs.reshape(lq, H_B, -1, HD_B).astype(jnp.float32))
    return o.reshape(lq, W_B)


def _prompt_seq(args, lam, subln_g, lambda_init):
    q_a, k_a, v_a, q_b, k_b, v_b = args
    L = q_a.shape[0]
    pos = jnp.arange(L, dtype=jnp.int32)
    o_a = _over_query_chunks(lambda qc, pc: _diff_chunk(qc, pc, k_a, v_a, lam, subln_g, lambda_init),
                             q_a, pos, min(DIFF_Q_BLOCK, L))
    kb, vb, means = _moba_blocks(k_b, v_b)
    o_b = _over_query_chunks(lambda qc, pc: _moba_chunk(qc, pc, kb, vb, means), q_b, pos, min(MOBA_Q_CHUNK, L))
    return o_a, o_b


def _sample_seq(args, cache_k_diff, cache_v_diff, cache_k_moba, cache_v_moba, layer, lam, subln_g, lambda_init):
    q_a, k_a, v_a, q_b, k_b, v_b, pages = args

    def past(cache):
        rows = cache[layer, pages]
        return rows.reshape((-1,) + rows.shape[2:])

    ka = jnp.concatenate([past(cache_k_diff), k_a], axis=0)
    va = jnp.concatenate([past(cache_v_diff), v_a], axis=0)
    kbf = jnp.concatenate([past(cache_k_moba), k_b], axis=0)
    vbf = jnp.concatenate([past(cache_v_moba), v_b], axis=0)
    pos = PAST_LEN + jnp.arange(q_a.shape[0], dtype=jnp.int32)
    o_a = _diff_chunk(q_a, pos, ka, va, lam, subln_g, lambda_init)
    kb, vb, means = _moba_blocks(kbf, vbf)
    o_b = _moba_chunk(q_b, pos, kb, vb, means)
    return o_a, o_b


def _hier_moe(h, w_grp, b_grp, w_exp, b_exp, w_gate, w_up, w_down):
    shp = h.shape
    hf = h.reshape(-1, D_MODEL)
    n = hf.shape[0]
    gp = jax.nn.softmax((hf @ w_grp).astype(jnp.float32) + b_grp.astype(jnp.float32), axis=-1)
    g_p, g_idx = lax.top_k(gp, 1)
    el = ((hf @ w_exp).astype(jnp.float32) + b_exp.astype(jnp.float32)).reshape(n, N_GROUPS, EXPERTS_PER_GROUP)
    el_g = jnp.take_along_axis(el, g_idx[:, :, None], axis=1)[:, 0]
    e_p, e_idx = lax.top_k(jax.nn.softmax(el_g, axis=-1), TOPK_IN_GROUP)
    e_p = e_p / jnp.sum(e_p, axis=-1, keepdims=True)
    wts = (g_p * e_p).reshape(-1)
    flat_e = (g_idx * EXPERTS_PER_GROUP + e_idx).reshape(-1)
    order = jnp.argsort(flat_e)
    tok = order // TOPK_IN_GROUP
    xs = hf[tok]
    sizes = jnp.bincount(flat_e, length=N_EXPERTS).astype(jnp.int32)
    a = lax.ragged_dot(xs, w_gate, sizes)
    b = lax.ragged_dot(xs, w_up, sizes)
    out = lax.ragged_dot(jax.nn.silu(a) * b, w_down, sizes)
    out = out * wts[order][:, None].astype(out.dtype)
    y = jnp.zeros_like(hf).at[tok].add(out)
    return y.reshape(shp)


def _merge_and_ffn(x, o_a, o_b, g_a, g_b, w_br_a, w_br_b, w_out, ln1_g, ln1_b,
                   w_grp, b_grp, w_exp, b_exp, w_gate, w_up, w_down, ln2_g, ln2_b):
    br_a = jnp.einsum('ble,ed->bld', o_a.astype(x.dtype), w_br_a)
    br_b = jnp.einsum('ble,ed->bld', o_b.astype(x.dtype), w_br_b)
    mix = jnp.einsum('bld,de->ble', g_a * br_a + g_b * br_b, w_out)
    h = _layer_norm(ALPHA * x + mix, ln1_g, ln1_b)
    f = _hier_moe(h, w_grp, b_grp, w_exp, b_exp, w_gate, w_up, w_down)
    return _layer_norm(ALPHA * h + f, ln2_g, ln2_b)


def setup_inputs(seed: int = 0) -> dict:
    key = jax.random.key(seed)
    ks = jax.random.split(key, 32)
    n_pages = PAST_LEN // PAGE_SIZE
    n_used = DEC_BATCH * n_pages
    n_pool = (n_used * 5) // 4
    f32 = jnp.float32

    def nrm(k, shape, scale):
        return jax.random.normal(k, shape, f32) * scale

    w_in = nrm(ks[0], (DEPTH, D_MODEL, W_IN_COLS), D_MODEL ** -0.5)
    v_cols = jnp.zeros((W_IN_COLS,), f32).at[2 * W_A:3 * W_A].set(1.0).at[3 * W_A + 2 * W_B:3 * W_A + 3 * W_B].set(1.0)
    w_in = w_in * jnp.where(v_cols > 0, BETA, 1.0)
    page_table = jax.random.permutation(ks[1], n_pool)[:n_used].reshape(DEC_BATCH, n_pages).astype(jnp.int32)
    return {
        'x_prompt': nrm(ks[2], (BATCH, SEQ, D_MODEL), 1.0),
        'x_sample': nrm(ks[3], (DEC_BATCH, DEC_SEQ, D_MODEL), 1.0),
        'cache_k_diff': nrm(ks[4], (DEPTH, n_pool, PAGE_SIZE, 2 * H_A, HD_A), 1.0),
        'cache_v_diff': nrm(ks[5], (DEPTH, n_pool, PAGE_SIZE, H_A, 2 * HD_A), 1.0),
        'cache_k_moba': nrm(ks[6], (DEPTH, n_pool, PAGE_SIZE, H_B, HD_B), 1.0),
        'cache_v_moba': nrm(ks[7], (DEPTH, n_pool, PAGE_SIZE, H_B, HD_B), 1.0),
        'page_table': page_table,
        'w_in': w_in,
        'w_br_a': nrm(ks[8], (DEPTH, W_A, D_MODEL), BETA * W_A ** -0.5),
        'w_br_b': nrm(ks[9], (DEPTH, W_B, D_MODEL), BETA * W_B ** -0.5),
        'w_out': nrm(ks[10], (DEPTH, D_MODEL, D_MODEL), BETA * D_MODEL ** -0.5),
        'diff_lambda': nrm(ks[11], (DEPTH, 4, HD_A), 0.1),
        'diff_subln': 1.0 + nrm(ks[12], (DEPTH, 2 * HD_A), 0.02),
        'ln1_g': 1.0 + nrm(ks[13], (DEPTH, D_MODEL), 0.02),
        'ln1_b': nrm(ks[14], (DEPTH, D_MODEL), 0.02),
        'w_grp': nrm(ks[15], (DEPTH, D_MODEL, N_GROUPS), D_MODEL ** -0.5),
        'b_grp': nrm(ks[16], (DEPTH, N_GROUPS), 0.01),
        'w_exp': nrm(ks[17], (DEPTH, D_MODEL, N_EXPERTS), D_MODEL ** -0.5),
        'b_exp': nrm(ks[18], (DEPTH, N_EXPERTS), 0.01),
        'w_gate': nrm(ks[19], (DEPTH, N_EXPERTS, D_MODEL, D_EXPERT), D_MODEL ** -0.5),
        'w_up': nrm(ks[20], (DEPTH, N_EXPERTS, D_MODEL, D_EXPERT), D_MODEL ** -0.5),
        'w_down': nrm(ks[21], (DEPTH, N_EXPERTS, D_EXPERT, D_MODEL), BETA * D_EXPERT ** -0.5),
        'ln2_g': 1.0 + nrm(ks[22], (DEPTH, D_MODEL), 0.02),
        'ln2_b': nrm(ks[23], (DEPTH, D_MODEL), 0.02),
    }


def reference(x_prompt, x_sample, cache_k_diff, cache_v_diff, cache_k_moba, cache_v_moba, page_table,
              w_in, w_br_a, w_br_b, w_out, diff_lambda, diff_subln, ln1_g, ln1_b,
              w_grp, b_grp, w_exp, b_exp, w_gate, w_up, w_down, ln2_g, ln2_b):
    xp, xs = x_prompt, x_sample
    pos_p = jnp.arange(x_prompt.shape[1], dtype=jnp.int32)
    pos_s = PAST_LEN + jnp.arange(x_sample.shape[1], dtype=jnp.int32)
    new_kdp, new_vdp, new_kmp, new_vmp = [], [], [], []
    new_kds, new_vds, new_kms, new_vms = [], [], [], []
    for l in range(DEPTH):
        lambda_init = 0.8 - 0.6 * math.exp(-0.3 * l)
        lp = diff_lambda[l].astype(jnp.float32)
        lam = jnp.exp(jnp.sum(lp[0] * lp[1])) - jnp.exp(jnp.sum(lp[2] * lp[3])) + lambda_init
        subln = diff_subln[l]
        ffn_args = (w_br_a[l], w_br_b[l], w_out[l], ln1_g[l], ln1_b[l], w_grp[l], b_grp[l], w_exp[l], b_exp[l],
                    w_gate[l], w_up[l], w_down[l], ln2_g[l], ln2_b[l])
        qa, ka, va, qb, kb, vb, ga, gb = _attn_inputs(xp, pos_p, w_in[l])
        oa, ob = lax.map(lambda a: _prompt_seq(a, lam, subln, lambda_init), (qa, ka, va, qb, kb, vb))
        new_kdp.append(ka); new_vdp.append(va); new_kmp.append(kb); new_vmp.append(vb)
        xp = _merge_and_ffn(xp, oa, ob, ga, gb, *ffn_args)
        qa, ka, va, qb, kb, vb, ga, gb = _attn_inputs(xs, pos_s, w_in[l])
        oa, ob = lax.map(lambda a: _sample_seq(a, cache_k_diff, cache_v_diff, cache_k_moba, cache_v_moba,
                                               l, lam, subln, lambda_init),
                         (qa, ka, va, qb, kb, vb, page_table))
        new_kds.append(ka); new_vds.append(va); new_kms.append(kb); new_vms.append(vb)
        xs = _merge_and_ffn(xs, oa, ob, ga, gb, *ffn_args)
    return (xp, xs,
            jnp.stack(new_kdp), jnp.stack(new_vdp), jnp.stack(new_kmp), jnp.stack(new_vmp),
            jnp.stack(new_kds), jnp.stack(new_vds), jnp.stack(new_kms), jnp.stack(new_vms))
```

```python
import functools
import math

import jax
import jax.numpy as jnp
import numpy as np
from jax import lax
from jax.experimental import pallas as pl
from jax.experimental.pallas import tpu as pltpu

F32 = jnp.float32
BF16 = jnp.bfloat16

LANES = 128
SUBLANES = 8

H_A = 4
HD_A = 64
H_B = 8
HD_B = 64
W_A = H_A * 2 * HD_A
W_B = H_B * HD_B
MOBA_BLOCK = 256
MOBA_TOPK = 3
ROPE_THETA = 500000.0
ROPE_FRACTION = 4
N_GROUPS = 4
EXPERTS_PER_GROUP = 8
N_EXPERTS = N_GROUPS * EXPERTS_PER_GROUP
TOPK_IN_GROUP = 2
LN_EPS = 1e-5
NEG = -1e30
SCALE_A = HD_A ** -0.5
SCALE_B = HD_B ** -0.5

VMEM_LIMIT = 56 * 1024 * 1024


def _cparams(sem):
    return pltpu.CompilerParams(dimension_semantics=sem, vmem_limit_bytes=VMEM_LIMIT)


def _rope_tables(pos, head_dim):
    rot = head_dim // ROPE_FRACTION
    half = rot // 2
    inv_freq = ROPE_THETA ** (-jnp.arange(half, dtype=F32) * 2.0 / rot)
    ang = pos.astype(F32)[:, None] * inv_freq[None, :]
    cos, sin = jnp.cos(ang), jnp.sin(ang)
    d = jnp.arange(LANES) % head_dim
    f = d % half
    cos_l = jnp.where(d[None, :] < rot, cos[:, f], 1.0)
    sin_lo = jnp.where(d[None, :] < half, -sin[:, f], 0.0)
    sin_hi = jnp.where((d[None, :] >= half) & (d[None, :] < rot), sin[:, f], 0.0)
    return cos_l.astype(F32), sin_lo.astype(F32), sin_hi.astype(F32), half


def _proj_kernel(x_ref, w_ref, cos_ref, slo_ref, shi_ref,
                 kd32_ref, vd32_ref, km32_ref, vm32_ref,
                 qd_ref, kd_ref, vd_ref, qm_ref, km_ref, vm_ref, ga_ref, gb_ref, *rest,
                 half, d_model, with_means):
    xb = x_ref[...].astype(BF16)
    cos = cos_ref[...]
    slo = slo_ref[...]
    shi = shi_ref[...]

    def mm(c0, width):
        return jnp.dot(xb, w_ref[:, c0:c0 + width], preferred_element_type=F32)

    def rope(p):
        return p * cos + pltpu.roll(p, LANES - half, axis=1) * slo + pltpu.roll(p, half, axis=1) * shi

    def rope_store(c0, width, scale, out32_ref, out16_ref):
        p = mm(c0, width)
        for j in range(width // LANES):
            r = rope(p[:, j * LANES:(j + 1) * LANES])
            if out32_ref is not None:
                out32_ref[:, j * LANES:(j + 1) * LANES] = r
            out16_ref[:, j * LANES:(j + 1) * LANES] = (r * scale).astype(BF16)
        return None

    c = 0
    rope_store(c, W_A, SCALE_A, None, qd_ref); c += W_A
    rope_store(c, W_A, 1.0, kd32_ref, kd_ref); c += W_A
    p = mm(c, W_A); vd32_ref[...] = p; vd_ref[...] = p.astype(BF16); c += W_A
    rope_store(c, W_B, SCALE_B, None, qm_ref); c += W_B
    rope_store(c, W_B, 1.0, km32_ref, km_ref)
    if with_means:
        kmean_ref = rest[0]
        tm = x_ref.shape[0]
        for blk in range(tm // MOBA_BLOCK):
            kb = km32_ref[blk * MOBA_BLOCK:(blk + 1) * MOBA_BLOCK, :]
            kmean_ref[blk] = jnp.sum(kb, axis=0, keepdims=True) * (1.0 / MOBA_BLOCK)
    c += W_B
    p = mm(c, W_B); vm32_ref[...] = p; vm_ref[...] = p.astype(BF16); c += W_B
    for out_ref in (ga_ref, gb_ref):
        for j in range(d_model // 512):
            out_ref[:, j * 512:(j + 1) * 512] = jax.nn.sigmoid(mm(c, 512))
            c += 512


def _proj(x2d, w_in16, cos, slo, shi, half, *, tm, with_means):
    n, d_model = x2d.shape
    n_tab = cos.shape[0]
    assert n % tm == 0 and n_tab % tm == 0
    tab_blocks = n_tab // tm
    cols = w_in16.shape[1]
    row = lambda i: (i, 0)
    tab = lambda i: (i % tab_blocks, 0)
    out_shape = [jax.ShapeDtypeStruct((n, 512), F32)] * 4 + [jax.ShapeDtypeStruct((n, 512), BF16)] * 6 \
        + [jax.ShapeDtypeStruct((n, d_model), F32)] * 2
    out_specs = [pl.BlockSpec((tm, 512), row)] * 10 + [pl.BlockSpec((tm, d_model), row)] * 2
    if with_means:
        assert tm % MOBA_BLOCK == 0
        out_shape.append(jax.ShapeDtypeStruct((n // MOBA_BLOCK, 1, W_B), F32))
        out_specs.append(pl.BlockSpec((tm // MOBA_BLOCK, 1, W_B), lambda i: (i, 0, 0)))
    return pl.pallas_call(
        functools.partial(_proj_kernel, half=half, d_model=d_model, with_means=with_means),
        grid=(n // tm,),
        in_specs=[pl.BlockSpec((tm, d_model), row),
                  pl.BlockSpec((d_model, cols), lambda i: (0, 0)),
                  pl.BlockSpec((tm, LANES), tab), pl.BlockSpec((tm, LANES), tab), pl.BlockSpec((tm, LANES), tab)],
        out_specs=out_specs,
        out_shape=out_shape,
        compiler_params=_cparams(("parallel",)),
    )(x2d, w_in16, cos, slo, shi)


def _stack_halves(q):
    lane = lax.broadcasted_iota(jnp.int32, q.shape, 1)
    zero = jnp.zeros_like(q)
    return jnp.concatenate([jnp.where(lane < LANES // 2, q, zero), jnp.where(lane >= LANES // 2, q, zero)], axis=0)


def _flash_update(s, v, m_ref, l_ref, acc_ref):
    m_old = m_ref[...]
    m_new = jnp.maximum(m_old, jnp.max(s, axis=1, keepdims=True))
    alpha = jnp.exp(m_old - m_new)
    p = jnp.exp(s - m_new)
    l_ref[...] = alpha * l_ref[...] + jnp.sum(p, axis=1, keepdims=True)
    acc_ref[...] = alpha * acc_ref[...] + jnp.dot(p.astype(BF16), v, preferred_element_type=F32)
    m_ref[...] = m_new


def _causal_mask(tq, tk):
    row = lax.broadcasted_iota(jnp.int32, (2 * tq, tk), 0)
    col = lax.broadcasted_iota(jnp.int32, (2 * tq, tk), 1)
    row = jnp.where(row >= tq, row - tq, row)
    return col <= row


_NT = (((1,), (1,)), ((), ()))


def _diff_lambda(lam_ref, lambda_init):
    lp = lam_ref[...]
    return (jnp.exp(jnp.sum(lp[0:1] * lp[1:2], axis=1, keepdims=True))
            - jnp.exp(jnp.sum(lp[2:3] * lp[3:4], axis=1, keepdims=True)) + lambda_init)


def _diff_finish(o1, o2, lam, sub, lambda_init):
    o = o1 - lam * o2
    o = o * lax.rsqrt(jnp.mean(o * o, axis=1, keepdims=True) + LN_EPS)
    return o * sub * (1.0 - lambda_init)


def _diff_prompt_kernel(lam_ref, sub_ref, q_ref, k_ref, v_ref, o_ref, m_ref, l_ref, acc_ref, *, tq, lambda_init):
    qi = pl.program_id(2)
    qs = _stack_halves(q_ref[0])
    m_ref[...] = jnp.full(m_ref.shape, NEG, F32)
    l_ref[...] = jnp.zeros(l_ref.shape, F32)
    acc_ref[...] = jnp.zeros(acc_ref.shape, F32)

    def scores(start):
        k = k_ref[0, pl.ds(start, tq), :]
        return lax.dot_general(qs, k, _NT, preferred_element_type=F32)

    diag = pl.multiple_of(qi * tq, tq)
    s = jnp.where(_causal_mask(tq, tq), scores(diag), NEG)
    _flash_update(s, v_ref[0, pl.ds(diag, tq), :], m_ref, l_ref, acc_ref)

    def body(j, carry):
        start = pl.multiple_of(j * tq, tq)
        _flash_update(scores(start), v_ref[0, pl.ds(start, tq), :], m_ref, l_ref, acc_ref)
        return carry

    lax.fori_loop(0, qi, body, 0)

    on = acc_ref[...] / l_ref[...]
    lam = _diff_lambda(lam_ref, lambda_init)
    o_ref[0] = _diff_finish(on[:tq], on[tq:], lam, sub_ref[...], lambda_init).astype(o_ref.dtype)


def _diff_prompt(q, k, v, diff_lambda, subln, lambda_init, *, tq):
    b, l, _ = q.shape
    assert l % tq == 0
    return pl.pallas_call(
        functools.partial(_diff_prompt_kernel, tq=tq, lambda_init=lambda_init),
        grid=(b, H_A, l // tq),
        in_specs=[pl.BlockSpec((4, HD_A), lambda bi, h, qi: (0, 0)),
                  pl.BlockSpec((1, LANES), lambda bi, h, qi: (0, 0)),
                  pl.BlockSpec((1, tq, LANES), lambda bi, h, qi: (bi, qi, h)),
                  pl.BlockSpec((1, l, LANES), lambda bi, h, qi: (bi, 0, h)),
                  pl.BlockSpec((1, l, LANES), lambda bi, h, qi: (bi, 0, h))],
        out_specs=pl.BlockSpec((1, tq, LANES), lambda bi, h, qi: (bi, qi, h)),
        out_shape=jax.ShapeDtypeStruct((b, l, W_A), BF16),
        scratch_shapes=[pltpu.VMEM((2 * tq, 1), F32), pltpu.VMEM((2 * tq, 1), F32),
                        pltpu.VMEM((2 * tq, LANES), F32)],
        compiler_params=_cparams(("parallel", "parallel", "arbitrary")),
    )(diff_lambda, subln, q, k, v)


def _moba_select(gate, n_past):
    blk = lax.broadcasted_iota(jnp.int32, gate.shape, 1).astype(F32)
    ninf = jnp.float32(-jnp.inf)
    g = jnp.where(blk < n_past, gate, ninf)
    bias = jnp.full(gate.shape, NEG, F32)
    for _ in range(MOBA_TOPK):
        mx = jnp.max(g, axis=1, keepdims=True)
        idx = jnp.min(jnp.where(g == mx, blk, float(LANES)), axis=1, keepdims=True)
        pick = (blk == idx) & (mx > ninf)
        bias = jnp.where(pick, 0.0, bias)
        g = jnp.where(pick, ninf, g)
    return bias


def _moba_prompt_kernel(q_ref, k_ref, v_ref, kmean_ref, o_ref, m_ref, l_ref, acc_ref):
    tq = MOBA_BLOCK
    qi = pl.program_id(2)
    qs = _stack_halves(q_ref[0])
    m_ref[...] = jnp.full(m_ref.shape, NEG, F32)
    l_ref[...] = jnp.zeros(l_ref.shape, F32)
    acc_ref[...] = jnp.zeros(acc_ref.shape, F32)

    own = pl.multiple_of(qi * tq, tq)
    s = lax.dot_general(qs, k_ref[0, pl.ds(own, tq), :], _NT, preferred_element_type=F32)
    s = jnp.where(_causal_mask(tq, tq), s, NEG)
    _flash_update(s, v_ref[0, pl.ds(own, tq), :], m_ref, l_ref, acc_ref)

    gate = lax.dot_general(qs, kmean_ref[0].astype(BF16), _NT, preferred_element_type=F32)
    bias = _moba_select(gate, qi.astype(F32))
    q_aug = jnp.concatenate([qs, bias.astype(BF16)], axis=1)
    lane = lax.broadcasted_iota(jnp.int32, (tq, LANES), 1)

    def body(j, carry):
        start = pl.multiple_of(j * tq, tq)
        onehot = jnp.where(lane == j, 1.0, 0.0).astype(BF16)
        k_aug = jnp.concatenate([k_ref[0, pl.ds(start, tq), :], onehot], axis=1)
        sj = lax.dot_general(q_aug, k_aug, _NT, preferred_element_type=F32)
        _flash_update(sj, v_ref[0, pl.ds(start, tq), :], m_ref, l_ref, acc_ref)
        return carry

    lax.fori_loop(0, qi, body, 0)

    on = acc_ref[...] / l_ref[...]
    lane_o = lax.broadcasted_iota(jnp.int32, (tq, LANES), 1)
    o_ref[0] = jnp.where(lane_o < LANES // 2, on[:tq], on[tq:]).astype(o_ref.dtype)


def _moba_prompt(q, k, v, kmean_pad):
    b, l, _ = q.shape
    tq = MOBA_BLOCK
    assert l % tq == 0 and l // tq <= LANES
    return pl.pallas_call(
        _moba_prompt_kernel,
        grid=(b, W_B // LANES, l // tq),
        in_specs=[pl.BlockSpec((1, tq, LANES), lambda bi, h, qi: (bi, qi, h)),
                  pl.BlockSpec((1, l, LANES), lambda bi, h, qi: (bi, 0, h)),
                  pl.BlockSpec((1, l, LANES), lambda bi, h, qi: (bi, 0, h)),
                  pl.BlockSpec((1, LANES, LANES), lambda bi, h, qi: (bi, 0, h))],
        out_specs=pl.BlockSpec((1, tq, LANES), lambda bi, h, qi: (bi, qi, h)),
        out_shape=jax.ShapeDtypeStruct((b, l, W_B), BF16),
        scratch_shapes=[pltpu.VMEM((2 * tq, 1), F32), pltpu.VMEM((2 * tq, 1), F32),
                        pltpu.VMEM((2 * tq, LANES), F32)],
        compiler_params=_cparams(("parallel", "parallel", "arbitrary")),
    )(q, k, v, kmean_pad)


def _layer_norm(z, g, b):
    mu = jnp.mean(z, axis=1, keepdims=True)
    zc = z - mu
    var = jnp.mean(zc * zc, axis=1, keepdims=True)
    return zc * lax.rsqrt(var + LN_EPS) * g + b


def _first_argmax(vals, lane_f, valid):
    ninf = jnp.float32(-jnp.inf)
    v = jnp.where(valid, vals, ninf)
    mx = jnp.max(v, axis=1, keepdims=True)
    idx = jnp.min(jnp.where(v == mx, lane_f, float(LANES)), axis=1, keepdims=True)
    return mx, idx


def _route(logits):
    lane_i = lax.broadcasted_iota(jnp.int32, logits.shape, 1)
    lane = lane_i.astype(F32)
    is_grp = lane_i < N_GROUPS
    gmax, gidx = _first_argmax(logits, lane, is_grp)
    gex = jnp.where(is_grp, jnp.exp(logits - gmax), 0.0)
    gp = gex / jnp.sum(gex, axis=1, keepdims=True)
    g_p, g_idx = _first_argmax(gp, lane, is_grp)
    lo = N_GROUPS + g_idx * EXPERTS_PER_GROUP
    in_grp = (lane >= lo) & (lane < lo + EXPERTS_PER_GROUP)
    emax, _ = _first_argmax(logits, lane, in_grp)
    eex = jnp.where(in_grp, jnp.exp(logits - emax), 0.0)
    ep = eex / jnp.sum(eex, axis=1, keepdims=True)
    p1, i1 = _first_argmax(ep, lane, in_grp)
    p2, i2 = _first_argmax(ep, lane, in_grp & (lane != i1))
    den = p1 + p2
    e1 = g_idx * EXPERTS_PER_GROUP + (i1 - lo)
    e2 = g_idx * EXPERTS_PER_GROUP + (i2 - lo)
    out = jnp.where(lane_i == 0, e1, 0.0)
    out = jnp.where(lane_i == 1, e2, out)
    out = jnp.where(lane_i == 2, g_p * (p1 / den), out)
    out = jnp.where(lane_i == 3, g_p * (p2 / den), out)
    return out


def _merge_kernel(x_ref, oa_ref, ob_ref, ga_ref, gb_ref, wa_ref, wb_ref, wo_ref, g1_ref, b1_ref,
                  wr_ref, br_ref, h_ref, route_ref, *, alpha):
    br_a = jnp.dot(oa_ref[...], wa_ref[...], preferred_element_type=F32)
    br_b = jnp.dot(ob_ref[...], wb_ref[...], preferred_element_type=F32)
    t = ga_ref[...] * br_a + gb_ref[...] * br_b
    mix = jnp.dot(t.astype(BF16), wo_ref[...], preferred_element_type=F32)
    h = _layer_norm(alpha * x_ref[...] + mix, g1_ref[...], b1_ref[...])
    h_ref[...] = h
    logits = jnp.dot(h.astype(BF16), wr_ref[...], preferred_element_type=F32) + br_ref[...]
    route_ref[...] = _route(logits)


def _merge(x2d, oa, ob, ga, gb, wa16, wb16, wo16, g1, b1, wr16, br, alpha, *, tm):
    n, d = x2d.shape
    assert n % tm == 0
    row = lambda i: (i, 0)
    fix = lambda i: (0, 0)
    return pl.pallas_call(
        functools.partial(_merge_kernel, alpha=alpha),
        grid=(n // tm,),
        in_specs=[pl.BlockSpec((tm, d), row), pl.BlockSpec((tm, W_A), row), pl.BlockSpec((tm, W_B), row),
                  pl.BlockSpec((tm, d), row), pl.BlockSpec((tm, d), row),
                  pl.BlockSpec((W_A, d), fix), pl.BlockSpec((W_B, d), fix), pl.BlockSpec((d, d), fix),
                  pl.BlockSpec((1, d), fix), pl.BlockSpec((1, d), fix),
                  pl.BlockSpec((d, LANES), fix), pl.BlockSpec((1, LANES), fix)],
        out_specs=[pl.BlockSpec((tm, d), row), pl.BlockSpec((tm, LANES), row)],
        out_shape=[jax.ShapeDtypeStruct((n, d), F32), jax.ShapeDtypeStruct((n, LANES), F32)],
        compiler_params=_cparams(("parallel",)),
    )(x2d, oa, ob, ga, gb, wa16, wb16, wo16, g1, b1, wr16, br)


def _gather_rows(idx_ref, n_rows, src_hbm, dst_vmem, sem):
    def row_copy(r):
        return pltpu.make_async_copy(src_hbm.at[pl.ds(idx_ref[r], 1), :], dst_vmem.at[pl.ds(r, 1), :], sem)

    def start(r, c):
        row_copy(r).start()
        return c

    def wait(r, c):
        row_copy(r).wait()
        return c

    lax.fori_loop(0, n_rows, start, 0)
    lax.fori_loop(0, n_rows, wait, 0)


def _expert_kernel(tile_e_ref, n_used_ref, src_ref, h_hbm, wg_ref, wu_ref, wd_ref, out_ref, xbuf, sem, *, tm):
    i = pl.program_id(0)

    @pl.when(i < n_used_ref[0])
    def _():
        _gather_rows(src_ref, tm, h_hbm, xbuf, sem)
        xb = xbuf[...].astype(BF16)
        a = jnp.dot(xb, wg_ref[0], preferred_element_type=F32)
        b = jnp.dot(xb, wu_ref[0], preferred_element_type=F32)
        act = (a * jax.nn.sigmoid(a)) * b
        out_ref[...] = jnp.dot(act.astype(BF16), wd_ref[0], preferred_element_type=F32)

    @pl.when(i >= n_used_ref[0])
    def _():
        out_ref[...] = jnp.zeros(out_ref.shape, F32)


def _experts(h, src_tok, tile_expert, n_used, wg16, wu16, wd16, *, tm):
    n, d = h.shape
    r_max = src_tok.shape[0]
    de = wg16.shape[2]
    n_tiles = r_max // tm
    grid_spec = pltpu.PrefetchScalarGridSpec(
        num_scalar_prefetch=2,
        grid=(n_tiles,),
        in_specs=[pl.BlockSpec((tm,), lambda i, te, nu: (i,), memory_space=pltpu.SMEM),
                  pl.BlockSpec(memory_space=pl.ANY),
                  pl.BlockSpec((1, d, de), lambda i, te, nu: (te[i], 0, 0)),
                  pl.BlockSpec((1, d, de), lambda i, te, nu: (te[i], 0, 0)),
                  pl.BlockSpec((1, de, d), lambda i, te, nu: (te[i], 0, 0))],
        out_specs=pl.BlockSpec((tm, d), lambda i, te, nu: (i, 0)),
        scratch_shapes=[pltpu.VMEM((tm, d), F32), pltpu.SemaphoreType.DMA(())],
    )
    return pl.pallas_call(
        functools.partial(_expert_kernel, tm=tm),
        grid_spec=grid_spec,
        out_shape=jax.ShapeDtypeStruct((r_max, d), F32),
        compiler_params=_cparams(("arbitrary",)),
    )(tile_expert, n_used, src_tok, h, wg16, wu16, wd16)


def _combine_kernel(pos0_ref, pos1_ref, h_ref, w_ref, g2_ref, b2_ref, out_hbm, y_ref, buf0, buf1, sem, *, tm, alpha):
    def copies(r):
        return (pltpu.make_async_copy(out_hbm.at[pl.ds(pos0_ref[r], 1), :], buf0.at[pl.ds(r, 1), :], sem),
                pltpu.make_async_copy(out_hbm.at[pl.ds(pos1_ref[r], 1), :], buf1.at[pl.ds(r, 1), :], sem))

    def start(r, c):
        c0, c1 = copies(r)
        c0.start()
        c1.start()
        return c

    def wait(r, c):
        c0, c1 = copies(r)
        c0.wait()
        c1.wait()
        return c

    lax.fori_loop(0, tm, start, 0)
    lax.fori_loop(0, tm, wait, 0)
    w = w_ref[...]
    f = buf0[...] * w[:, 2:3] + buf1[...] * w[:, 3:4]
    y_ref[...] = _layer_norm(alpha * h_ref[...] + f, g2_ref[...], b2_ref[...])


def _combine(h, route, pos0, pos1, out_sorted, g2, b2, alpha, *, tm):
    n, d = h.shape
    assert n % tm == 0
    row = lambda i: (i, 0)
    fix = lambda i: (0, 0)
    return pl.pallas_call(
        functools.partial(_combine_kernel, tm=tm, alpha=alpha),
        grid=(n // tm,),
        in_specs=[pl.BlockSpec((tm,), lambda i: (i,), memory_space=pltpu.SMEM),
                  pl.BlockSpec((tm,), lambda i: (i,), memory_space=pltpu.SMEM),
                  pl.BlockSpec((tm, d), row), pl.BlockSpec((tm, LANES), row),
                  pl.BlockSpec((1, d), fix), pl.BlockSpec((1, d), fix),
                  pl.BlockSpec(memory_space=pl.ANY)],
        out_specs=pl.BlockSpec((tm, d), row),
        out_shape=jax.ShapeDtypeStruct((n, d), F32),
        scratch_shapes=[pltpu.VMEM((tm, d), F32), pltpu.VMEM((tm, d), F32), pltpu.SemaphoreType.DMA(())],
        compiler_params=_cparams(("arbitrary",)),
    )(pos0, pos1, h, route, g2, b2, out_sorted)


def _moe_plan(route, tm):
    n = route.shape[0]
    flat_e = route[:, :TOPK_IN_GROUP].astype(jnp.int32).reshape(-1)
    n_asg = flat_e.shape[0]
    order = jnp.argsort(flat_e, stable=True)
    e_sorted = flat_e[order]
    sizes = jnp.bincount(flat_e, length=N_EXPERTS).astype(jnp.int32)
    padded = ((sizes + tm - 1) // tm) * tm
    pstart = jnp.cumsum(padded) - padded
    start = jnp.cumsum(sizes) - sizes
    dest = pstart[e_sorted] + (jnp.arange(n_asg, dtype=jnp.int32) - start[e_sorted])
    r_max = ((n_asg + N_EXPERTS * (tm - 1)) // tm + 1) * tm
    src_tok = jnp.zeros((r_max,), jnp.int32).at[dest].set((order // TOPK_IN_GROUP).astype(jnp.int32))
    pos = jnp.zeros((n_asg,), jnp.int32).at[order].set(dest.astype(jnp.int32)).reshape(n, TOPK_IN_GROUP)
    pend = jnp.cumsum(padded)
    tile_start = jnp.arange(r_max // tm, dtype=jnp.int32) * tm
    tile_expert = jnp.minimum(jnp.searchsorted(pend, tile_start, side='right'), N_EXPERTS - 1).astype(jnp.int32)
    n_used = (pend[-1] // tm).astype(jnp.int32).reshape(1)
    return src_tok, tile_expert, n_used, pos[:, 0], pos[:, 1]


ROWS_PER_HEAD = SUBLANES
N_SUB = 8
QROWS = N_SUB * ROWS_PER_HEAD


def _block_diag_queries(q):
    s, t, w = q.shape
    qp = jnp.pad(q, ((0, 0), (0, ROWS_PER_HEAD - t), (0, 0)))
    tiled = jnp.tile(qp, (1, N_SUB, 1))
    row_head = jnp.arange(QROWS)[:, None] // ROWS_PER_HEAD
    lane_head = jnp.arange(w)[None, :] // (w // N_SUB)
    return jnp.where((row_head == lane_head)[None], tiled, jnp.zeros_like(tiled))


def _new_token_mask(n_new):
    row = lax.broadcasted_iota(jnp.int32, (QROWS, LANES), 0) % ROWS_PER_HEAD
    col = lax.broadcasted_iota(jnp.int32, (QROWS, LANES), 1)
    return (col <= row) & (col < n_new)


def _diff_sample_kernel(pt_ref, lam_ref, sub_ref, q_ref, kn_ref, vn_ref, *refs, pages, n_new, lambda_init):
    k_refs = refs[:pages]
    v_refs = refs[pages:2 * pages]
    o_ref, m_ref, l_ref, acc_ref = refs[2 * pages:]
    c = pl.program_id(1)
    q = q_ref[0]

    @pl.when(c == 0)
    def _():
        m_ref[...] = jnp.full(m_ref.shape, NEG, F32)
        l_ref[...] = jnp.zeros(l_ref.shape, F32)
        acc_ref[...] = jnp.zeros(acc_ref.shape, F32)
        s = lax.dot_general(q, kn_ref[0], _NT, preferred_element_type=F32)
        s = jnp.where(_new_token_mask(n_new), s, NEG)
        _flash_update(s, vn_ref[0], m_ref, l_ref, acc_ref)

    s = jnp.concatenate(
        [lax.dot_general(q, k_refs[j][0].astype(BF16), _NT, preferred_element_type=F32) for j in range(pages)],
        axis=1)
    v = jnp.concatenate([v_refs[j][0].astype(BF16) for j in range(pages)], axis=0)
    _flash_update(s, v, m_ref, l_ref, acc_ref)

    @pl.when(c == pl.num_programs(1) - 1)
    def _():
        on = acc_ref[...] / l_ref[...]
        lam = _diff_lambda(lam_ref, lambda_init)
        for h in range(H_A):
            o1 = on[(2 * h) * ROWS_PER_HEAD:(2 * h + 1) * ROWS_PER_HEAD, h * LANES:(h + 1) * LANES]
            o2 = on[(2 * h + 1) * ROWS_PER_HEAD:(2 * h + 2) * ROWS_PER_HEAD, h * LANES:(h + 1) * LANES]
            o_ref[0, :, h * LANES:(h + 1) * LANES] = _diff_finish(o1, o2, lam, sub_ref[...], lambda_init).astype(o_ref.dtype)


def _page_specs(pages, page, width, n_chunks, phase):
    def make(j):
        def index_map(b, c, pt):
            if phase == 0:
                c = jnp.minimum(c, n_chunks - 1)
            elif phase == 1:
                c = jnp.maximum(c - n_chunks, 0)
            return (pt[b, c * pages + j], 0, 0)
        return pl.BlockSpec((1, page, width), index_map)
    return [make(j) for j in range(pages)]


def _diff_sample(page_table, diff_lambda, subln, qbd, k_new, v_new, cache_k, cache_v, lambda_init, *, pages, n_new):
    s, n_pages = page_table.shape
    _, page, width = cache_k.shape
    assert n_pages % pages == 0
    n_chunks = n_pages // pages
    per_seq = lambda b, c, pt: (b, 0, 0)
    fix = lambda b, c, pt: (0, 0)
    grid_spec = pltpu.PrefetchScalarGridSpec(
        num_scalar_prefetch=1,
        grid=(s, n_chunks),
        in_specs=[pl.BlockSpec((4, HD_A), fix), pl.BlockSpec((1, LANES), fix),
                  pl.BlockSpec((1, QROWS, width), per_seq),
                  pl.BlockSpec((1, LANES, width), per_seq), pl.BlockSpec((1, LANES, width), per_seq)]
        + _page_specs(pages, page, width, n_chunks, None) + _page_specs(pages, page, width, n_chunks, None),
        out_specs=pl.BlockSpec((1, ROWS_PER_HEAD, width), per_seq),
        scratch_shapes=[pltpu.VMEM((QROWS, 1), F32), pltpu.VMEM((QROWS, 1), F32), pltpu.VMEM((QROWS, width), F32)],
    )
    return pl.pallas_call(
        functools.partial(_diff_sample_kernel, pages=pages, n_new=n_new, lambda_init=lambda_init),
        grid_spec=grid_spec,
        out_shape=jax.ShapeDtypeStruct((s, ROWS_PER_HEAD, width), BF16),
        compiler_params=_cparams(("parallel", "arbitrary")),
    )(page_table, diff_lambda, subln, qbd, k_new, v_new, *([cache_k] * pages), *([cache_v] * pages))


def _moba_sample_kernel(pt_ref, q_ref, kn_ref, vn_ref, *refs, pages, page, n_chunks, n_new):
    k_refs = refs[:pages]
    v_refs = refs[pages:2 * pages]
    o_ref, s_ref, p_ref, gate_ref, l_ref, acc_ref = refs[2 * pages:]
    c = pl.program_id(1)
    q = q_ref[0]
    span = pages * page
    blocks_per_page = MOBA_BLOCK // page
    lane = lax.broadcasted_iota(jnp.int32, (QROWS, LANES), 1)

    @pl.when(c == 0)
    def _():
        gate_ref[...] = jnp.zeros(gate_ref.shape, F32)

    @pl.when(c < n_chunks)
    def _():
        g = gate_ref[...]
        for j in range(pages):
            sj = lax.dot_general(q, k_refs[j][0].astype(BF16), _NT, preferred_element_type=F32)
            s_ref[:, pl.ds(pl.multiple_of(c * span + j * page, page), page)] = sj
            blk = (c * pages + j) // blocks_per_page
            g = g + jnp.where(lane == blk, jnp.sum(sj, axis=1, keepdims=True) * (1.0 / MOBA_BLOCK), 0.0)
        gate_ref[...] = g

    @pl.when(c == n_chunks - 1)
    def _():
        n_past = (n_chunks * span) // MOBA_BLOCK
        bias = _moba_select(gate_ref[...], float(n_past))
        s_new = lax.dot_general(q, kn_ref[0], _NT, preferred_element_type=F32)
        s_new = jnp.where(_new_token_mask(n_new), s_new, NEG)
        m = jnp.max(s_new, axis=1, keepdims=True)
        bias16 = bias.astype(BF16)
        brow = lax.broadcasted_iota(jnp.int32, (LANES, span), 0)
        bcol = lax.broadcasted_iota(jnp.int32, (LANES, span), 1)
        for cc in range(n_chunks):
            expand = jnp.where((bcol + cc * span) // MOBA_BLOCK == brow, 1.0, 0.0).astype(BF16)
            sc = s_ref[:, cc * span:(cc + 1) * span] + jnp.dot(bias16, expand, preferred_element_type=F32)
            s_ref[:, cc * span:(cc + 1) * span] = sc
            m = jnp.maximum(m, jnp.max(sc, axis=1, keepdims=True))
        p_new = jnp.exp(s_new - m)
        l = jnp.sum(p_new, axis=1, keepdims=True)
        for cc in range(n_chunks):
            pc = jnp.exp(s_ref[:, cc * span:(cc + 1) * span] - m)
            l = l + jnp.sum(pc, axis=1, keepdims=True)
            p_ref[:, cc * span:(cc + 1) * span] = pc.astype(BF16)
        l_ref[...] = l
        acc_ref[...] = jnp.dot(p_new.astype(BF16), vn_ref[0], preferred_element_type=F32)

    @pl.when(c >= n_chunks)
    def _():
        cv = c - n_chunks
        acc = acc_ref[...]
        for j in range(pages):
            pj = p_ref[:, pl.ds(pl.multiple_of(cv * span + j * page, page), page)]
            acc = acc + jnp.dot(pj, v_refs[j][0].astype(BF16), preferred_element_type=F32)
        acc_ref[...] = acc

    @pl.when(c == 2 * n_chunks - 1)
    def _():
        on = acc_ref[...] / l_ref[...]
        lane_o = lax.broadcasted_iota(jnp.int32, (ROWS_PER_HEAD, on.shape[1]), 1) // HD_B
        out = jnp.zeros((ROWS_PER_HEAD, on.shape[1]), F32)
        for h in range(H_B):
            out = jnp.where(lane_o == h, on[h * ROWS_PER_HEAD:(h + 1) * ROWS_PER_HEAD, :], out)
        o_ref[0] = out.astype(o_ref.dtype)


def _moba_sample(page_table, qbd, k_new, v_new, cache_k, cache_v, *, pages, n_new):
    s, n_pages = page_table.shape
    _, page, width = cache_k.shape
    assert n_pages % pages == 0 and MOBA_BLOCK % page == 0 and (n_pages * page) % MOBA_BLOCK == 0
    assert (n_pages * page) // MOBA_BLOCK <= LANES
    n_chunks = n_pages // pages
    past = n_pages * page
    per_seq = lambda b, c, pt: (b, 0, 0)
    grid_spec = pltpu.PrefetchScalarGridSpec(
        num_scalar_prefetch=1,
        grid=(s, 2 * n_chunks),
        in_specs=[pl.BlockSpec((1, QROWS, width), per_seq),
                  pl.BlockSpec((1, LANES, width), per_seq), pl.BlockSpec((1, LANES, width), per_seq)]
        + _page_specs(pages, page, width, n_chunks, 0) + _page_specs(pages, page, width, n_chunks, 1),
        out_specs=pl.BlockSpec((1, ROWS_PER_HEAD, width), per_seq),
        scratch_shapes=[pltpu.VMEM((QROWS, past), F32), pltpu.VMEM((QROWS, past), BF16),
                        pltpu.VMEM((QROWS, LANES), F32), pltpu.VMEM((QROWS, 1), F32),
                        pltpu.VMEM((QROWS, width), F32)],
    )
    return pl.pallas_call(
        functools.partial(_moba_sample_kernel, pages=pages, page=page, n_chunks=n_chunks, n_new=n_new),
        grid_spec=grid_spec,
        out_shape=jax.ShapeDtypeStruct((s, ROWS_PER_HEAD, width), BF16),
        compiler_params=_cparams(("parallel", "arbitrary")),
    )(page_table, qbd, k_new, v_new, *([cache_k] * pages), *([cache_v] * pages))


def _pad_rows(a, rows):
    return jnp.pad(a, ((0, 0), (0, rows - a.shape[1]), (0, 0)))


def _ffn(x2d, oa, ob, ga, gb, wts, alpha, *, tm_merge, tm_exp, tm_comb):
    h, route = _merge(x2d, oa, ob, ga, gb, wts['wa'], wts['wb'], wts['wo'], wts['g1'], wts['b1'],
                      wts['wr'], wts['br'], alpha, tm=tm_merge)
    src_tok, tile_expert, n_used, pos0, pos1 = _moe_plan(route, tm_exp)
    out_sorted = _experts(h, src_tok, tile_expert, n_used, wts['wg'], wts['wu'], wts['wd'], tm=tm_exp)
    return _combine(h, route, pos0, pos1, out_sorted, wts['g2'], wts['b2'], alpha, tm=tm_comb)


def kernel(x_prompt, x_sample, cache_k_diff, cache_v_diff, cache_k_moba, cache_v_moba, page_table,
           w_in, w_br_a, w_br_b, w_out, diff_lambda, diff_subln, ln1_g, ln1_b,
           w_grp, b_grp, w_exp, b_exp, w_gate, w_up, w_down, ln2_g, ln2_b):
    depth = w_in.shape[0]
    assert depth == 1, "kernel written for a single layer"
    b, l, d = x_prompt.shape
    s, t, _ = x_sample.shape
    n_pool, page = cache_k_diff.shape[1], cache_k_diff.shape[2]
    past_len = page_table.shape[1] * page
    alpha = (2.0 * depth) ** 0.25
    lyr = 0
    lambda_init = 0.8 - 0.6 * math.exp(-0.3 * lyr)

    w_in16 = w_in[lyr].astype(BF16)
    n_route = N_GROUPS + N_EXPERTS
    wr = jnp.concatenate([w_grp[lyr], w_exp[lyr], jnp.zeros((d, LANES - n_route), F32)], axis=1).astype(BF16)
    br = jnp.concatenate([b_grp[lyr], b_exp[lyr], jnp.zeros((LANES - n_route,), F32)]).reshape(1, LANES)
    wts = dict(wa=w_br_a[lyr].astype(BF16), wb=w_br_b[lyr].astype(BF16), wo=w_out[lyr].astype(BF16),
               g1=ln1_g[lyr].reshape(1, d), b1=ln1_b[lyr].reshape(1, d), wr=wr, br=br,
               wg=w_gate[lyr].astype(BF16), wu=w_up[lyr].astype(BF16), wd=w_down[lyr].astype(BF16),
               g2=ln2_g[lyr].reshape(1, d), b2=ln2_b[lyr].reshape(1, d))
    lam_p = diff_lambda[lyr]
    subln = diff_subln[lyr].reshape(1, 2 * HD_A)

    cos, slo, shi, half = _rope_tables(jnp.arange(l, dtype=jnp.int32), HD_A)
    tm = min(512, l)
    xp2 = x_prompt.reshape(b * l, d)
    (kd32, vd32, km32, vm32, qd, kd, vd, qm, km, vm, ga, gb, kmean) = _proj(
        xp2, w_in16, cos, slo, shi, half, tm=tm, with_means=True)
    r3 = lambda a: a.reshape(b, l, a.shape[-1])
    oa = _diff_prompt(r3(qd), r3(kd), r3(vd), lam_p, subln, lambda_init, tq=min(256, l))
    nb = l // MOBA_BLOCK
    kmean_pad = jnp.pad(kmean.reshape(b, nb, W_B), ((0, 0), (0, LANES - nb), (0, 0)))
    ob = _moba_prompt(r3(qm), r3(km), r3(vm), kmean_pad)
    y_p = _ffn(xp2, oa.reshape(b * l, W_A), ob.reshape(b * l, W_B), ga, gb, wts, alpha,
               tm_merge=tm, tm_exp=256, tm_comb=256).reshape(b, l, d)
    kd_p = kd32.reshape(1, b, l, 2 * H_A, HD_A)
    vd_p = vd32.reshape(1, b, l, H_A, 2 * HD_A)
    km_p = km32.reshape(1, b, l, H_B, HD_B)
    vm_p = vm32.reshape(1, b, l, H_B, HD_B)

    pos_s = past_len + (jnp.arange(s * t, dtype=jnp.int32) % t)
    cos, slo, shi, half = _rope_tables(pos_s, HD_A)
    xs2 = x_sample.reshape(s * t, d)
    (kd32, vd32, km32, vm32, qd, kd, vd, qm, km, vm, ga, gb) = _proj(
        xs2, w_in16, cos, slo, shi, half, tm=s * t, with_means=False)
    r3 = lambda a: a.reshape(s, t, a.shape[-1])
    pages = 8
    ck = lambda cch: cch[lyr].reshape(n_pool, page, -1)
    oa = _diff_sample(page_table, lam_p, subln, _block_diag_queries(r3(qd)), _pad_rows(r3(kd), LANES),
                      _pad_rows(r3(vd), LANES), ck(cache_k_diff), ck(cache_v_diff), lambda_init, pages=pages, n_new=t)
    ob = _moba_sample(page_table, _block_diag_queries(r3(qm)), _pad_rows(r3(km), LANES), _pad_rows(r3(vm), LANES),
                      ck(cache_k_moba), ck(cache_v_moba), pages=pages, n_new=t)
    oa = oa[:, :t].reshape(s * t, W_A)
    ob = ob[:, :t].reshape(s * t, W_B)
    y_s = _ffn(xs2, oa, ob, ga, gb, wts, alpha, tm_merge=s * t, tm_exp=256,
               tm_comb=min(256, s * t)).reshape(s, t, d)
    kd_s = kd32.reshape(1, s, t, 2 * H_A, HD_A)
    vd_s = vd32.reshape(1, s, t, H_A, 2 * HD_A)
    km_s = km32.reshape(1, s, t, H_B, HD_B)
    vm_s = vm32.reshape(1, s, t, H_B, HD_B)

    return (y_p, y_s, kd_p, vd_p, km_p, vm_p, kd_s, vd_s, km_s, vm_s)
```

```python
import functools
import math

import jax
import jax.numpy as jnp
import numpy as np
from jax import lax
from jax.experimental import pallas as pl
from jax.experimental.pallas import tpu as pltpu

F32 = jnp.float32
BF16 = jnp.bfloat16

LANES = 128
SUBLANES = 8

H_A = 4
HD_A = 64
H_B = 8
HD_B = 64
W_A = H_A * 2 * HD_A
W_B = H_B * HD_B
MOBA_BLOCK = 256
MOBA_TOPK = 3
ROPE_THETA = 500000.0
ROPE_FRACTION = 4
N_GROUPS = 4
EXPERTS_PER_GROUP = 8
N_EXPERTS = N_GROUPS * EXPERTS_PER_GROUP
TOPK_IN_GROUP = 2
LN_EPS = 1e-5
NEG = -1e30
SCALE_A = HD_A ** -0.5
SCALE_B = HD_B ** -0.5

VMEM_LIMIT = 56 * 1024 * 1024
KEY_CHUNK = 1024
DIFF_ROW_SPLITS = 4
MOBA_ROW_SPLITS = 2


def _cparams(sem):
    return pltpu.CompilerParams(dimension_semantics=sem, vmem_limit_bytes=VMEM_LIMIT)


def _rope_tables(pos, head_dim):
    rot = head_dim // ROPE_FRACTION
    half = rot // 2
    inv_freq = ROPE_THETA ** (-jnp.arange(half, dtype=F32) * 2.0 / rot)
    ang = pos.astype(F32)[:, None] * inv_freq[None, :]
    cos, sin = jnp.cos(ang), jnp.sin(ang)
    d = jnp.arange(LANES) % head_dim
    f = d % half
    cos_l = jnp.where(d[None, :] < rot, cos[:, f], 1.0)
    sin_lo = jnp.where(d[None, :] < half, -sin[:, f], 0.0)
    sin_hi = jnp.where((d[None, :] >= half) & (d[None, :] < rot), sin[:, f], 0.0)
    return cos_l.astype(F32), sin_lo.astype(F32), sin_hi.astype(F32), half


def _proj_kernel(x_ref, w_ref, cos_ref, slo_ref, shi_ref,
                 kd32_ref, vd32_ref, km32_ref, vm32_ref,
                 qd_ref, kd_ref, vd_ref, qm_ref, km_ref, vm_ref, ga_ref, gb_ref, *rest,
                 half, d_model, with_means):
    xb = x_ref[...].astype(BF16)
    cos = cos_ref[...]
    slo = slo_ref[...]
    shi = shi_ref[...]

    def mm(c0, width):
        return jnp.dot(xb, w_ref[:, c0:c0 + width], preferred_element_type=F32)

    def rope(p):
        return p * cos + pltpu.roll(p, LANES - half, axis=1) * slo + pltpu.roll(p, half, axis=1) * shi

    def rope_store(c0, width, scale, out32_ref, out16_ref):
        p = mm(c0, width)
        for j in range(width // LANES):
            r = rope(p[:, j * LANES:(j + 1) * LANES])
            if out32_ref is not None:
                out32_ref[:, j * LANES:(j + 1) * LANES] = r
            out16_ref[:, j * LANES:(j + 1) * LANES] = (r * scale).astype(BF16)
        return None

    c = 0
    rope_store(c, W_A, SCALE_A, None, qd_ref); c += W_A
    rope_store(c, W_A, 1.0, kd32_ref, kd_ref); c += W_A
    p = mm(c, W_A); vd32_ref[...] = p; vd_ref[...] = p.astype(BF16); c += W_A
    rope_store(c, W_B, SCALE_B, None, qm_ref); c += W_B
    rope_store(c, W_B, 1.0, km32_ref, km_ref)
    if with_means:
        kmean_ref = rest[0]
        tm = x_ref.shape[0]
        for blk in range(tm // MOBA_BLOCK):
            kb = km32_ref[blk * MOBA_BLOCK:(blk + 1) * MOBA_BLOCK, :]
            kmean_ref[blk] = jnp.sum(kb, axis=0, keepdims=True) * (1.0 / MOBA_BLOCK)
    c += W_B
    p = mm(c, W_B); vm32_ref[...] = p; vm_ref[...] = p.astype(BF16); c += W_B
    for out_ref in (ga_ref, gb_ref):
        for j in range(d_model // 512):
            out_ref[:, j * 512:(j + 1) * 512] = jax.nn.sigmoid(mm(c, 512))
            c += 512


def _proj(x2d, w_in16, cos, slo, shi, half, *, tm, with_means):
    n, d_model = x2d.shape
    n_tab = cos.shape[0]
    assert n % tm == 0 and n_tab % tm == 0
    tab_blocks = n_tab // tm
    cols = w_in16.shape[1]
    row = lambda i: (i, 0)
    tab = lambda i: (i % tab_blocks, 0)
    out_shape = [jax.ShapeDtypeStruct((n, 512), F32)] * 4 + [jax.ShapeDtypeStruct((n, 512), BF16)] * 6 \
        + [jax.ShapeDtypeStruct((n, d_model), F32)] * 2
    out_specs = [pl.BlockSpec((tm, 512), row)] * 10 + [pl.BlockSpec((tm, d_model), row)] * 2
    if with_means:
        assert tm % MOBA_BLOCK == 0
        out_shape.append(jax.ShapeDtypeStruct((n // MOBA_BLOCK, 1, W_B), F32))
        out_specs.append(pl.BlockSpec((tm // MOBA_BLOCK, 1, W_B), lambda i: (i, 0, 0)))
    return pl.pallas_call(
        functools.partial(_proj_kernel, half=half, d_model=d_model, with_means=with_means),
        grid=(n // tm,),
        in_specs=[pl.BlockSpec((tm, d_model), row),
                  pl.BlockSpec((d_model, cols), lambda i: (0, 0)),
                  pl.BlockSpec((tm, LANES), tab), pl.BlockSpec((tm, LANES), tab), pl.BlockSpec((tm, LANES), tab)],
        out_specs=out_specs,
        out_shape=out_shape,
        compiler_params=_cparams(("parallel",)),
    )(x2d, w_in16, cos, slo, shi)


def _stack_halves(q):
    lane = lax.broadcasted_iota(jnp.int32, q.shape, 1)
    zero = jnp.zeros_like(q)
    return jnp.concatenate([jnp.where(lane < LANES // 2, q, zero), jnp.where(lane >= LANES // 2, q, zero)], axis=0)


def _flash_update(s, v, m_ref, l_ref, acc_ref):
    m_old = m_ref[...]
    m_new = jnp.maximum(m_old, jnp.max(s, axis=1, keepdims=True))
    alpha = jnp.exp(m_old - m_new)
    p = jnp.exp(s - m_new)
    l_ref[...] = alpha * l_ref[...] + jnp.sum(p, axis=1, keepdims=True)
    acc_ref[...] = alpha * acc_ref[...] + jnp.dot(p.astype(BF16), v, preferred_element_type=F32)
    m_ref[...] = m_new


def _causal_mask(tq, tk):
    row = lax.broadcasted_iota(jnp.int32, (2 * tq, tk), 0)
    col = lax.broadcasted_iota(jnp.int32, (2 * tq, tk), 1)
    row = jnp.where(row >= tq, row - tq, row)
    return col <= row


_NT = (((1,), (1,)), ((), ()))


def _diff_lambda(lam_ref, lambda_init):
    lp = lam_ref[...]
    return (jnp.exp(jnp.sum(lp[0:1] * lp[1:2], axis=1, keepdims=True))
            - jnp.exp(jnp.sum(lp[2:3] * lp[3:4], axis=1, keepdims=True)) + lambda_init)


def _diff_finish(o1, o2, lam, sub, lambda_init):
    o = o1 - lam * o2
    o = o * lax.rsqrt(jnp.mean(o * o, axis=1, keepdims=True) + LN_EPS)
    return o * sub * (1.0 - lambda_init)


def _flash_chunk(s, v_aug, m_ref, acc_ref):
    groups = s.shape[1] // LANES
    part = s[:, :LANES]
    for g in range(1, groups):
        part = jnp.maximum(part, s[:, g * LANES:(g + 1) * LANES])
    m_old = m_ref[...]
    m_new = jnp.maximum(m_old, jnp.max(part, axis=1, keepdims=True))
    alpha = jnp.exp(m_old - m_new)
    p = jnp.exp(s - jnp.concatenate([m_new] * groups, axis=1))
    acc_ref[...] = acc_ref[...] * jnp.concatenate([alpha, alpha], axis=1) \
        + jnp.dot(p.astype(BF16), v_aug, preferred_element_type=F32)
    m_ref[...] = m_new


def _attend_chunk(q, k, v_aug, m_ref, acc_ref, *, tq, q_start, k_start, masked, splits):
    rows, ck = q.shape[0], k.shape[0]
    rs = rows // splits
    assert tq % rs == 0
    for i in range(splits):
        s = lax.dot_general(q[i * rs:(i + 1) * rs], k, _NT, preferred_element_type=F32)
        if masked:
            row = lax.broadcasted_iota(jnp.int32, (rs, ck), 0) + (i * rs) % tq
            col = lax.broadcasted_iota(jnp.int32, (rs, ck), 1)
            s = jnp.where(col + k_start <= row + q_start, s, NEG)
        _flash_chunk(s, v_aug, m_ref.at[pl.ds(i * rs, rs)], acc_ref.at[pl.ds(i * rs, rs)])


def _diff_prompt_kernel(lam_ref, sub_ref, q_ref, k_ref, v_ref, o_ref, m_ref, acc_ref, *, tq, ck, lambda_init):
    qi = pl.program_id(2)
    qs = _stack_halves(q_ref[0])
    m_ref[...] = jnp.full(m_ref.shape, NEG, F32)
    acc_ref[...] = jnp.zeros(acc_ref.shape, F32)
    ones = jnp.ones((ck, LANES), BF16)
    q_start = qi * tq
    n_full = q_start // ck

    def chunk(start, masked):
        v_aug = jnp.concatenate([v_ref[0, pl.ds(start, ck), :], ones], axis=1)
        _attend_chunk(qs, k_ref[0, pl.ds(start, ck), :], v_aug, m_ref, acc_ref,
                      tq=tq, q_start=q_start, k_start=start, masked=masked, splits=DIFF_ROW_SPLITS)

    chunk(pl.multiple_of(n_full * ck, ck), True)

    def body(j, carry):
        chunk(pl.multiple_of(j * ck, ck), False)
        return carry

    lax.fori_loop(0, n_full, body, 0)

    acc = acc_ref[...]
    on = acc[:, :LANES] / acc[:, LANES:]
    lam = _diff_lambda(lam_ref, lambda_init)
    o_ref[0] = _diff_finish(on[:tq], on[tq:], lam, sub_ref[...], lambda_init).astype(o_ref.dtype)


def _diff_prompt(q, k, v, diff_lambda, subln, lambda_init, *, tq, ck):
    b, l, _ = q.shape
    assert l % ck == 0 and ck % tq == 0
    return pl.pallas_call(
        functools.partial(_diff_prompt_kernel, tq=tq, ck=ck, lambda_init=lambda_init),
        name="diff_prompt",
        grid=(b, H_A, l // tq),
        in_specs=[pl.BlockSpec((4, HD_A), lambda bi, h, qi: (0, 0)),
                  pl.BlockSpec((1, LANES), lambda bi, h, qi: (0, 0)),
                  pl.BlockSpec((1, tq, LANES), lambda bi, h, qi: (bi, qi, h)),
                  pl.BlockSpec((1, l, LANES), lambda bi, h, qi: (bi, 0, h)),
                  pl.BlockSpec((1, l, LANES), lambda bi, h, qi: (bi, 0, h))],
        out_specs=pl.BlockSpec((1, tq, LANES), lambda bi, h, qi: (bi, qi, h)),
        out_shape=jax.ShapeDtypeStruct((b, l, W_A), BF16),
        scratch_shapes=[pltpu.VMEM((2 * tq, LANES), F32), pltpu.VMEM((2 * tq, 2 * LANES), F32)],
        compiler_params=_cparams(("parallel", "parallel", "arbitrary")),
    )(diff_lambda, subln, q, k, v)


def _moba_select(gate, n_past, own=None):
    blk = lax.broadcasted_iota(jnp.int32, gate.shape, 1).astype(F32)
    ninf = jnp.float32(-jnp.inf)
    g = jnp.where(blk < n_past, gate, ninf)
    bias = jnp.full(gate.shape, NEG, F32)
    if own is not None:
        bias = jnp.where(blk == own, 0.0, bias)
    for _ in range(MOBA_TOPK):
        mx = jnp.max(g, axis=1, keepdims=True)
        idx = jnp.min(jnp.where(g == mx, blk, float(LANES)), axis=1, keepdims=True)
        pick = (blk == idx) & (mx > ninf)
        bias = jnp.where(pick, 0.0, bias)
        g = jnp.where(pick, ninf, g)
    return bias


def _moba_prompt_kernel(q_ref, k_ref, v_ref, kmean_ref, o_ref, m_ref, acc_ref, *, ck):
    tq = MOBA_BLOCK
    qi = pl.program_id(2)
    qs = _stack_halves(q_ref[0])
    m_ref[...] = jnp.full(m_ref.shape, NEG, F32)
    acc_ref[...] = jnp.zeros(acc_ref.shape, F32)
    ones = jnp.ones((ck, LANES), BF16)
    q_start = qi * tq
    n_full = q_start // ck

    gate = lax.dot_general(qs, kmean_ref[0].astype(BF16), _NT, preferred_element_type=F32)
    bias = _moba_select(gate, qi.astype(F32), own=qi.astype(F32))
    q_aug = jnp.concatenate([qs, bias.astype(BF16)], axis=1)
    lane = lax.broadcasted_iota(jnp.int32, (ck, LANES), 1)
    row_blk = lax.broadcasted_iota(jnp.int32, (ck, LANES), 0) // MOBA_BLOCK

    def chunk(start, masked):
        onehot = jnp.where(lane == row_blk + start // MOBA_BLOCK, 1.0, 0.0).astype(BF16)
        k_aug = jnp.concatenate([k_ref[0, pl.ds(start, ck), :], onehot], axis=1)
        v_aug = jnp.concatenate([v_ref[0, pl.ds(start, ck), :], ones], axis=1)
        _attend_chunk(q_aug, k_aug, v_aug, m_ref, acc_ref, tq=tq, q_start=q_start, k_start=start, masked=masked,
                      splits=MOBA_ROW_SPLITS)

    chunk(pl.multiple_of(n_full * ck, ck), True)

    def body(j, carry):
        chunk(pl.multiple_of(j * ck, ck), False)
        return carry

    lax.fori_loop(0, n_full, body, 0)

    acc = acc_ref[...]
    on = acc[:, :LANES] / acc[:, LANES:]
    lane_o = lax.broadcasted_iota(jnp.int32, (tq, LANES), 1)
    o_ref[0] = jnp.where(lane_o < LANES // 2, on[:tq], on[tq:]).astype(o_ref.dtype)


def _moba_prompt(q, k, v, kmean_pad, *, ck):
    b, l, _ = q.shape
    tq = MOBA_BLOCK
    assert l % ck == 0 and ck % tq == 0 and l // tq <= LANES
    return pl.pallas_call(
        functools.partial(_moba_prompt_kernel, ck=ck),
        name="moba_prompt",
        grid=(b, W_B // LANES, l // tq),
        in_specs=[pl.BlockSpec((1, tq, LANES), lambda bi, h, qi: (bi, qi, h)),
                  pl.BlockSpec((1, l, LANES), lambda bi, h, qi: (bi, 0, h)),
                  pl.BlockSpec((1, l, LANES), lambda bi, h, qi: (bi, 0, h)),
                  pl.BlockSpec((1, LANES, LANES), lambda bi, h, qi: (bi, 0, h))],
        out_specs=pl.BlockSpec((1, tq, LANES), lambda bi, h, qi: (bi, qi, h)),
        out_shape=jax.ShapeDtypeStruct((b, l, W_B), BF16),
        scratch_shapes=[pltpu.VMEM((2 * tq, LANES), F32), pltpu.VMEM((2 * tq, 2 * LANES), F32)],
        compiler_params=_cparams(("parallel", "parallel", "arbitrary")),
    )(q, k, v, kmean_pad)


def _layer_norm(z, g, b):
    mu = jnp.mean(z, axis=1, keepdims=True)
    zc = z - mu
    var = jnp.mean(zc * zc, axis=1, keepdims=True)
    return zc * lax.rsqrt(var + LN_EPS) * g + b


def _first_argmax(vals, lane_f, valid):
    ninf = jnp.float32(-jnp.inf)
    v = jnp.where(valid, vals, ninf)
    mx = jnp.max(v, axis=1, keepdims=True)
    idx = jnp.min(jnp.where(v == mx, lane_f, float(LANES)), axis=1, keepdims=True)
    return mx, idx


def _route(logits):
    lane_i = lax.broadcasted_iota(jnp.int32, logits.shape, 1)
    lane = lane_i.astype(F32)
    is_grp = lane_i < N_GROUPS
    gmax, gidx = _first_argmax(logits, lane, is_grp)
    gex = jnp.where(is_grp, jnp.exp(logits - gmax), 0.0)
    gp = gex / jnp.sum(gex, axis=1, keepdims=True)
    g_p, g_idx = _first_argmax(gp, lane, is_grp)
    lo = N_GROUPS + g_idx * EXPERTS_PER_GROUP
    in_grp = (lane >= lo) & (lane < lo + EXPERTS_PER_GROUP)
    emax, _ = _first_argmax(logits, lane, in_grp)
    eex = jnp.where(in_grp, jnp.exp(logits - emax), 0.0)
    ep = eex / jnp.sum(eex, axis=1, keepdims=True)
    p1, i1 = _first_argmax(ep, lane, in_grp)
    p2, i2 = _first_argmax(ep, lane, in_grp & (lane != i1))
    den = p1 + p2
    e1 = g_idx * EXPERTS_PER_GROUP + (i1 - lo)
    e2 = g_idx * EXPERTS_PER_GROUP + (i2 - lo)
    out = jnp.where(lane_i == 0, e1, 0.0)
    out = jnp.where(lane_i == 1, e2, out)
    out = jnp.where(lane_i == 2, g_p * (p1 / den), out)
    out = jnp.where(lane_i == 3, g_p * (p2 / den), out)
    return out


def _merge_kernel(x_ref, oa_ref, ob_ref, ga_ref, gb_ref, wa_ref, wb_ref, wo_ref, g1_ref, b1_ref,
                  wr_ref, br_ref, h_ref, route_ref, *, alpha):
    br_a = jnp.dot(oa_ref[...], wa_ref[...], preferred_element_type=F32)
    br_b = jnp.dot(ob_ref[...], wb_ref[...], preferred_element_type=F32)
    t = ga_ref[...] * br_a + gb_ref[...] * br_b
    mix = jnp.dot(t.astype(BF16), wo_ref[...], preferred_element_type=F32)
    h = _layer_norm(alpha * x_ref[...] + mix, g1_ref[...], b1_ref[...])
    h_ref[...] = h
    logits = jnp.dot(h.astype(BF16), wr_ref[...], preferred_element_type=F32) + br_ref[...]
    route_ref[...] = _route(logits)


def _merge(x2d, oa, ob, ga, gb, wa16, wb16, wo16, g1, b1, wr16, br, alpha, *, tm):
    n, d = x2d.shape
    assert n % tm == 0
    row = lambda i: (i, 0)
    fix = lambda i: (0, 0)
    return pl.pallas_call(
        functools.partial(_merge_kernel, alpha=alpha),
        grid=(n // tm,),
        in_specs=[pl.BlockSpec((tm, d), row), pl.BlockSpec((tm, W_A), row), pl.BlockSpec((tm, W_B), row),
                  pl.BlockSpec((tm, d), row), pl.BlockSpec((tm, d), row),
                  pl.BlockSpec((W_A, d), fix), pl.BlockSpec((W_B, d), fix), pl.BlockSpec((d, d), fix),
                  pl.BlockSpec((1, d), fix), pl.BlockSpec((1, d), fix),
                  pl.BlockSpec((d, LANES), fix), pl.BlockSpec((1, LANES), fix)],
        out_specs=[pl.BlockSpec((tm, d), row), pl.BlockSpec((tm, LANES), row)],
        out_shape=[jax.ShapeDtypeStruct((n, d), F32), jax.ShapeDtypeStruct((n, LANES), F32)],
        compiler_params=_cparams(("parallel",)),
    )(x2d, oa, ob, ga, gb, wa16, wb16, wo16, g1, b1, wr16, br)


GATHER_UNROLL = 8


def _row_copy(idx_ref, r, src_hbm, dst_vmem, sem):
    return pltpu.make_async_copy(src_hbm.at[pl.ds(idx_ref[r], 1), :], dst_vmem.at[pl.ds(r, 1), :], sem)


def _gather_start(idx_ref, src_hbm, dst_vmem, sem):
    def body(g, c):
        for u in range(GATHER_UNROLL):
            _row_copy(idx_ref, g * GATHER_UNROLL + u, src_hbm, dst_vmem, sem).start()
        return c
    lax.fori_loop(0, dst_vmem.shape[0] // GATHER_UNROLL, body, 0)


def _gather_wait(idx_ref, src_hbm, dst_vmem, sem):
    def body(g, c):
        for u in range(GATHER_UNROLL):
            _row_copy(idx_ref, g * GATHER_UNROLL + u, src_hbm, dst_vmem, sem).wait()
        return c
    lax.fori_loop(0, dst_vmem.shape[0] // GATHER_UNROLL, body, 0)


def _expert_kernel(tile_e_ref, n_used_ref, src_ref, nxt_ref, h_hbm, wg_ref, wu_ref, wd_ref, out_ref, xbuf, sem):
    i = pl.program_id(0)
    n_used = n_used_ref[0]
    slot = i % 2

    @pl.when((i == 0) & (i < n_used))
    def _():
        _gather_start(src_ref, h_hbm, xbuf.at[0], sem.at[0])

    @pl.when(i + 1 < n_used)
    def _():
        _gather_start(nxt_ref, h_hbm, xbuf.at[1 - slot], sem.at[1 - slot])

    @pl.when(i < n_used)
    def _():
        _gather_wait(src_ref, h_hbm, xbuf.at[slot], sem.at[slot])
        xb = xbuf[slot].astype(BF16)
        a = jnp.dot(xb, wg_ref[0], preferred_element_type=F32)
        b = jnp.dot(xb, wu_ref[0], preferred_element_type=F32)
        act = (a * jax.nn.sigmoid(a)) * b
        out_ref[...] = jnp.dot(act.astype(BF16), wd_ref[0], preferred_element_type=F32)

    @pl.when(i >= n_used)
    def _():
        out_ref[...] = jnp.zeros(out_ref.shape, F32)


def _experts(h, src_tok, tile_expert, n_used, wg16, wu16, wd16, *, tm):
    n, d = h.shape
    r_max = src_tok.shape[0]
    de = wg16.shape[2]
    n_tiles = r_max // tm
    assert tm % GATHER_UNROLL == 0
    grid_spec = pltpu.PrefetchScalarGridSpec(
        num_scalar_prefetch=2,
        grid=(n_tiles,),
        in_specs=[pl.BlockSpec((tm,), lambda i, te, nu: (i,), memory_space=pltpu.SMEM),
                  pl.BlockSpec((tm,), lambda i, te, nu: (jnp.minimum(i + 1, n_tiles - 1),), memory_space=pltpu.SMEM),
                  pl.BlockSpec(memory_space=pl.ANY),
                  pl.BlockSpec((1, d, de), lambda i, te, nu: (te[i], 0, 0)),
                  pl.BlockSpec((1, d, de), lambda i, te, nu: (te[i], 0, 0)),
                  pl.BlockSpec((1, de, d), lambda i, te, nu: (te[i], 0, 0))],
        out_specs=pl.BlockSpec((tm, d), lambda i, te, nu: (i, 0)),
        scratch_shapes=[pltpu.VMEM((2, tm, d), F32), pltpu.SemaphoreType.DMA((2,))],
    )
    return pl.pallas_call(
        _expert_kernel,
        name="experts",
        grid_spec=grid_spec,
        out_shape=jax.ShapeDtypeStruct((r_max, d), F32),
        compiler_params=_cparams(("arbitrary",)),
    )(tile_expert, n_used, src_tok, src_tok, h, wg16, wu16, wd16)


def _combine_kernel(pos0_ref, pos1_ref, nxt0_ref, nxt1_ref, h_ref, w_ref, g2_ref, b2_ref, out_hbm, y_ref,
                    buf0, buf1, sem0, sem1, *, alpha):
    i = pl.program_id(0)
    slot = i % 2

    @pl.when(i == 0)
    def _():
        _gather_start(pos0_ref, out_hbm, buf0.at[0], sem0.at[0])
        _gather_start(pos1_ref, out_hbm, buf1.at[0], sem1.at[0])

    @pl.when(i + 1 < pl.num_programs(0))
    def _():
        _gather_start(nxt0_ref, out_hbm, buf0.at[1 - slot], sem0.at[1 - slot])
        _gather_start(nxt1_ref, out_hbm, buf1.at[1 - slot], sem1.at[1 - slot])

    _gather_wait(pos0_ref, out_hbm, buf0.at[slot], sem0.at[slot])
    _gather_wait(pos1_ref, out_hbm, buf1.at[slot], sem1.at[slot])
    w = w_ref[...]
    f = buf0[slot] * w[:, 2:3] + buf1[slot] * w[:, 3:4]
    y_ref[...] = _layer_norm(alpha * h_ref[...] + f, g2_ref[...], b2_ref[...])


def _combine(h, route, pos0, pos1, out_sorted, g2, b2, alpha, *, tm):
    n, d = h.shape
    assert n % tm == 0 and tm % GATHER_UNROLL == 0
    n_tiles = n // tm
    row = lambda i: (i, 0)
    fix = lambda i: (0, 0)
    cur = pl.BlockSpec((tm,), lambda i: (i,), memory_space=pltpu.SMEM)
    nxt = pl.BlockSpec((tm,), lambda i: (jnp.minimum(i + 1, n_tiles - 1),), memory_space=pltpu.SMEM)
    return pl.pallas_call(
        functools.partial(_combine_kernel, alpha=alpha),
        name="combine",
        grid=(n_tiles,),
        in_specs=[cur, cur, nxt, nxt,
                  pl.BlockSpec((tm, d), row), pl.BlockSpec((tm, LANES), row),
                  pl.BlockSpec((1, d), fix), pl.BlockSpec((1, d), fix),
                  pl.BlockSpec(memory_space=pl.ANY)],
        out_specs=pl.BlockSpec((tm, d), row),
        out_shape=jax.ShapeDtypeStruct((n, d), F32),
        scratch_shapes=[pltpu.VMEM((2, tm, d), F32), pltpu.VMEM((2, tm, d), F32),
                        pltpu.SemaphoreType.DMA((2,)), pltpu.SemaphoreType.DMA((2,))],
        compiler_params=_cparams(("arbitrary",)),
    )(pos0, pos1, pos0, pos1, h, route, g2, b2, out_sorted)


def _moe_plan(route, tm):
    n = route.shape[0]
    flat_e = route[:, :TOPK_IN_GROUP].astype(jnp.int32).reshape(-1)
    n_asg = flat_e.shape[0]
    order = jnp.argsort(flat_e, stable=True).astype(jnp.int32)
    rank = jnp.argsort(order).astype(jnp.int32)
    experts = jnp.arange(N_EXPERTS, dtype=jnp.int32)
    sizes = jnp.sum((flat_e[:, None] == experts[None, :]).astype(jnp.int32), axis=0)
    padded = ((sizes + tm - 1) // tm) * tm
    pend = jnp.cumsum(padded)
    pstart = pend - padded
    start = jnp.cumsum(sizes) - sizes
    pos = (pstart[flat_e] + (rank - start[flat_e])).reshape(n, TOPK_IN_GROUP)
    r_max = ((n_asg + N_EXPERTS * (tm - 1)) // tm + 1) * tm
    tile_start = jnp.arange(r_max // tm, dtype=jnp.int32) * tm
    tile_expert = jnp.minimum(jnp.sum((tile_start[:, None] >= pend[None, :]).astype(jnp.int32), axis=1),
                              N_EXPERTS - 1)
    n_used = (pend[-1] // tm).astype(jnp.int32).reshape(1)
    row = jnp.arange(r_max, dtype=jnp.int32)
    row_e = jnp.repeat(tile_expert, tm)
    off = row - pstart[row_e]
    slot = jnp.clip(start[row_e] + off, 0, n_asg - 1)
    src_tok = jnp.where(off < sizes[row_e], order[slot] // TOPK_IN_GROUP, 0).astype(jnp.int32)
    return src_tok, tile_expert, n_used, pos[:, 0], pos[:, 1]


ROWS_PER_HEAD = SUBLANES
N_SUB = 8
QROWS = N_SUB * ROWS_PER_HEAD


def _block_diag_queries(q):
    s, t, w = q.shape
    qp = jnp.pad(q, ((0, 0), (0, ROWS_PER_HEAD - t), (0, 0)))
    tiled = jnp.tile(qp, (1, N_SUB, 1))
    row_head = jnp.arange(QROWS)[:, None] // ROWS_PER_HEAD
    lane_head = jnp.arange(w)[None, :] // (w // N_SUB)
    return jnp.where((row_head == lane_head)[None], tiled, jnp.zeros_like(tiled))


def _new_token_mask(n_new):
    row = lax.broadcasted_iota(jnp.int32, (QROWS, LANES), 0) % ROWS_PER_HEAD
    col = lax.broadcasted_iota(jnp.int32, (QROWS, LANES), 1)
    return (col <= row) & (col < n_new)


def _diff_sample_kernel(pt_ref, lam_ref, sub_ref, q_ref, kn_ref, vn_ref, *refs, pages, n_new, lambda_init):
    k_refs = refs[:pages]
    v_refs = refs[pages:2 * pages]
    o_ref, m_ref, l_ref, acc_ref = refs[2 * pages:]
    c = pl.program_id(1)
    q = q_ref[0]

    @pl.when(c == 0)
    def _():
        m_ref[...] = jnp.full(m_ref.shape, NEG, F32)
        l_ref[...] = jnp.zeros(l_ref.shape, F32)
        acc_ref[...] = jnp.zeros(acc_ref.shape, F32)
        s = lax.dot_general(q, kn_ref[0], _NT, preferred_element_type=F32)
        s = jnp.where(_new_token_mask(n_new), s, NEG)
        _flash_update(s, vn_ref[0], m_ref, l_ref, acc_ref)

    kt = jnp.concatenate([k_refs[j][0].reshape(q.shape[1], -1).astype(BF16) for j in range(pages)], axis=1)
    s = jnp.dot(q, kt, preferred_element_type=F32)
    v = jnp.concatenate(
        [jnp.concatenate([v_refs[j][0, :, h, :] for h in range(H_A)], axis=1).astype(BF16) for j in range(pages)],
        axis=0)
    _flash_update(s, v, m_ref, l_ref, acc_ref)

    @pl.when(c == pl.num_programs(1) - 1)
    def _():
        on = acc_ref[...] / l_ref[...]
        lam = _diff_lambda(lam_ref, lambda_init)
        for h in range(H_A):
            o1 = on[(2 * h) * ROWS_PER_HEAD:(2 * h + 1) * ROWS_PER_HEAD, h * LANES:(h + 1) * LANES]
            o2 = on[(2 * h + 1) * ROWS_PER_HEAD:(2 * h + 2) * ROWS_PER_HEAD, h * LANES:(h + 1) * LANES]
            o_ref[0, :, h * LANES:(h + 1) * LANES] = _diff_finish(o1, o2, lam, sub_ref[...], lambda_init).astype(o_ref.dtype)


def _page_specs(pages, page_shape, n_chunks, phase):
    def make(j):
        def index_map(b, c, pt):
            if phase == 0:
                c = jnp.minimum(c, n_chunks - 1)
            elif phase == 1:
                c = jnp.maximum(c - n_chunks, 0)
            return (pt[b, c * pages + j],) + (0,) * len(page_shape)
        return pl.BlockSpec((1,) + tuple(page_shape), index_map)
    return [make(j) for j in range(pages)]


def _diff_sample(page_table, diff_lambda, subln, qbd, k_new, v_new, cache_kt, cache_v, lambda_init, *, pages, n_new):
    s, n_pages = page_table.shape
    width = qbd.shape[2]
    page = cache_kt.shape[3]
    assert n_pages % pages == 0
    n_chunks = n_pages // pages
    per_seq = lambda b, c, pt: (b, 0, 0)
    fix = lambda b, c, pt: (0, 0)
    grid_spec = pltpu.PrefetchScalarGridSpec(
        num_scalar_prefetch=1,
        grid=(s, n_chunks),
        in_specs=[pl.BlockSpec((4, HD_A), fix), pl.BlockSpec((1, LANES), fix),
                  pl.BlockSpec((1, QROWS, width), per_seq),
                  pl.BlockSpec((1, LANES, width), per_seq), pl.BlockSpec((1, LANES, width), per_seq)]
        + _page_specs(pages, cache_kt.shape[1:], n_chunks, None) + _page_specs(pages, cache_v.shape[1:], n_chunks, None),
        out_specs=pl.BlockSpec((1, ROWS_PER_HEAD, width), per_seq),
        scratch_shapes=[pltpu.VMEM((QROWS, 1), F32), pltpu.VMEM((QROWS, 1), F32), pltpu.VMEM((QROWS, width), F32)],
    )
    return pl.pallas_call(
        functools.partial(_diff_sample_kernel, pages=pages, n_new=n_new, lambda_init=lambda_init),
        name="diff_sample",
        grid_spec=grid_spec,
        out_shape=jax.ShapeDtypeStruct((s, ROWS_PER_HEAD, width), BF16),
        compiler_params=_cparams(("parallel", "arbitrary")),
    )(page_table, diff_lambda, subln, qbd, k_new, v_new, *([cache_kt] * pages), *([cache_v] * pages))


def _moba_sample_kernel(pt_ref, q_ref, kn_ref, vn_ref, *refs, pages, page, n_chunks, n_new):
    k_refs = refs[:pages]
    v_refs = refs[pages:2 * pages]
    o_ref, s_ref, p_ref, gate_ref, l_ref, acc_ref = refs[2 * pages:]
    c = pl.program_id(1)
    q = q_ref[0]
    span = pages * page
    blocks_per_page = MOBA_BLOCK // page
    lane = lax.broadcasted_iota(jnp.int32, (QROWS, LANES), 1)

    @pl.when(c == 0)
    def _():
        gate_ref[...] = jnp.zeros(gate_ref.shape, F32)

    @pl.when(c < n_chunks)
    def _():
        g = gate_ref[...]
        kt = jnp.concatenate([k_refs[j][0].reshape(q.shape[1], -1).astype(BF16) for j in range(pages)], axis=1)
        s_chunk = jnp.dot(q, kt, preferred_element_type=F32)
        s_ref[:, pl.ds(pl.multiple_of(c * span, span), span)] = s_chunk
        for j in range(pages):
            sj = s_chunk[:, j * page:(j + 1) * page]
            blk = (c * pages + j) // blocks_per_page
            g = g + jnp.where(lane == blk, jnp.sum(sj, axis=1, keepdims=True) * (1.0 / MOBA_BLOCK), 0.0)
        gate_ref[...] = g

    @pl.when(c == n_chunks - 1)
    def _():
        n_past = (n_chunks * span) // MOBA_BLOCK
        bias = _moba_select(gate_ref[...], float(n_past))
        s_new = lax.dot_general(q, kn_ref[0], _NT, preferred_element_type=F32)
        s_new = jnp.where(_new_token_mask(n_new), s_new, NEG)
        m = jnp.max(s_new, axis=1, keepdims=True)
        bias16 = bias.astype(BF16)
        brow = lax.broadcasted_iota(jnp.int32, (LANES, span), 0)
        bcol = lax.broadcasted_iota(jnp.int32, (LANES, span), 1)
        for cc in range(n_chunks):
            expand = jnp.where((bcol + cc * span) // MOBA_BLOCK == brow, 1.0, 0.0).astype(BF16)
            sc = s_ref[:, cc * span:(cc + 1) * span] + jnp.dot(bias16, expand, preferred_element_type=F32)
            s_ref[:, cc * span:(cc + 1) * span] = sc
            m = jnp.maximum(m, jnp.max(sc, axis=1, keepdims=True))
        p_new = jnp.exp(s_new - m)
        l = jnp.sum(p_new, axis=1, keepdims=True)
        for cc in range(n_chunks):
            pc = jnp.exp(s_ref[:, cc * span:(cc + 1) * span] - m)
            l = l + jnp.sum(pc, axis=1, keepdims=True)
            p_ref[:, cc * span:(cc + 1) * span] = pc.astype(BF16)
        l_ref[...] = l
        acc_ref[...] = jnp.dot(p_new.astype(BF16), vn_ref[0], preferred_element_type=F32)

    @pl.when(c >= n_chunks)
    def _():
        cv = c - n_chunks
        acc = acc_ref[...]
        for j in range(pages):
            pj = p_ref[:, pl.ds(pl.multiple_of(cv * span + j * page, page), page)]
            vt = v_refs[j][0].reshape(acc.shape[1], -1).astype(BF16)
            acc = acc + lax.dot_general(pj, vt, _NT, preferred_element_type=F32)
        acc_ref[...] = acc

    @pl.when(c == 2 * n_chunks - 1)
    def _():
        on = acc_ref[...] / l_ref[...]
        lane_o = lax.broadcasted_iota(jnp.int32, (ROWS_PER_HEAD, on.shape[1]), 1) // HD_B
        out = jnp.zeros((ROWS_PER_HEAD, on.shape[1]), F32)
        for h in range(H_B):
            out = jnp.where(lane_o == h, on[h * ROWS_PER_HEAD:(h + 1) * ROWS_PER_HEAD, :], out)
        o_ref[0] = out.astype(o_ref.dtype)


def _moba_sample(page_table, qbd, k_new, v_new, cache_kt, cache_vt, *, pages, n_new):
    s, n_pages = page_table.shape
    width = qbd.shape[2]
    page = cache_kt.shape[3]
    assert n_pages % pages == 0 and MOBA_BLOCK % page == 0 and (n_pages * page) % MOBA_BLOCK == 0
    assert (n_pages * page) // MOBA_BLOCK <= LANES
    n_chunks = n_pages // pages
    past = n_pages * page
    per_seq = lambda b, c, pt: (b, 0, 0)
    grid_spec = pltpu.PrefetchScalarGridSpec(
        num_scalar_prefetch=1,
        grid=(s, 2 * n_chunks),
        in_specs=[pl.BlockSpec((1, QROWS, width), per_seq),
                  pl.BlockSpec((1, LANES, width), per_seq), pl.BlockSpec((1, LANES, width), per_seq)]
        + _page_specs(pages, cache_kt.shape[1:], n_chunks, 0) + _page_specs(pages, cache_vt.shape[1:], n_chunks, 1),
        out_specs=pl.BlockSpec((1, ROWS_PER_HEAD, width), per_seq),
        scratch_shapes=[pltpu.VMEM((QROWS, past), F32), pltpu.VMEM((QROWS, past), BF16),
                        pltpu.VMEM((QROWS, LANES), F32), pltpu.VMEM((QROWS, 1), F32),
                        pltpu.VMEM((QROWS, width), F32)],
    )
    return pl.pallas_call(
        functools.partial(_moba_sample_kernel, pages=pages, page=page, n_chunks=n_chunks, n_new=n_new),
        name="moba_sample",
        grid_spec=grid_spec,
        out_shape=jax.ShapeDtypeStruct((s, ROWS_PER_HEAD, width), BF16),
        compiler_params=_cparams(("parallel", "arbitrary")),
    )(page_table, qbd, k_new, v_new, *([cache_kt] * pages), *([cache_vt] * pages))


def _pad_rows(a, rows):
    return jnp.pad(a, ((0, 0), (0, rows - a.shape[1]), (0, 0)))


def _ffn(x2d, oa, ob, ga, gb, wts, alpha, *, tm_merge, tm_exp, tm_comb):
    h, route = _merge(x2d, oa, ob, ga, gb, wts['wa'], wts['wb'], wts['wo'], wts['g1'], wts['b1'],
                      wts['wr'], wts['br'], alpha, tm=tm_merge)
    src_tok, tile_expert, n_used, pos0, pos1 = _moe_plan(route, tm_exp)
    out_sorted = _experts(h, src_tok, tile_expert, n_used, wts['wg'], wts['wu'], wts['wd'], tm=tm_exp)
    return _combine(h, route, pos0, pos1, out_sorted, wts['g2'], wts['b2'], alpha, tm=tm_comb)


def kernel(x_prompt, x_sample, cache_k_diff, cache_v_diff, cache_k_moba, cache_v_moba, page_table,
           w_in, w_br_a, w_br_b, w_out, diff_lambda, diff_subln, ln1_g, ln1_b,
           w_grp, b_grp, w_exp, b_exp, w_gate, w_up, w_down, ln2_g, ln2_b):
    depth = w_in.shape[0]
    assert depth == 1, "kernel written for a single layer"
    b, l, d = x_prompt.shape
    s, t, _ = x_sample.shape
    n_pool, page = cache_k_diff.shape[1], cache_k_diff.shape[2]
    past_len = page_table.shape[1] * page
    alpha = (2.0 * depth) ** 0.25
    lyr = 0
    lambda_init = 0.8 - 0.6 * math.exp(-0.3 * lyr)

    w_in16 = w_in[lyr].astype(BF16)
    n_route = N_GROUPS + N_EXPERTS
    wr = jnp.concatenate([w_grp[lyr], w_exp[lyr], jnp.zeros((d, LANES - n_route), F32)], axis=1).astype(BF16)
    br = jnp.concatenate([b_grp[lyr], b_exp[lyr], jnp.zeros((LANES - n_route,), F32)]).reshape(1, LANES)
    wts = dict(wa=w_br_a[lyr].astype(BF16), wb=w_br_b[lyr].astype(BF16), wo=w_out[lyr].astype(BF16),
               g1=ln1_g[lyr].reshape(1, d), b1=ln1_b[lyr].reshape(1, d), wr=wr, br=br,
               wg=w_gate[lyr].astype(BF16), wu=w_up[lyr].astype(BF16), wd=w_down[lyr].astype(BF16),
               g2=ln2_g[lyr].reshape(1, d), b2=ln2_b[lyr].reshape(1, d))
    lam_p = diff_lambda[lyr]
    subln = diff_subln[lyr].reshape(1, 2 * HD_A)

    cos, slo, shi, half = _rope_tables(jnp.arange(l, dtype=jnp.int32), HD_A)
    tm = min(512, l)
    xp2 = x_prompt.reshape(b * l, d)
    (kd32, vd32, km32, vm32, qd, kd, vd, qm, km, vm, ga, gb, kmean) = _proj(
        xp2, w_in16, cos, slo, shi, half, tm=tm, with_means=True)
    r3 = lambda a: a.reshape(b, l, a.shape[-1])
    ck = min(KEY_CHUNK, l)
    oa = _diff_prompt(r3(qd), r3(kd), r3(vd), lam_p, subln, lambda_init, tq=min(256, l), ck=ck)
    nb = l // MOBA_BLOCK
    kmean_pad = jnp.pad(kmean.reshape(b, nb, W_B), ((0, 0), (0, LANES - nb), (0, 0)))
    ob = _moba_prompt(r3(qm), r3(km), r3(vm), kmean_pad, ck=ck)
    y_p = _ffn(xp2, oa.reshape(b * l, W_A), ob.reshape(b * l, W_B), ga, gb, wts, alpha,
               tm_merge=tm, tm_exp=256, tm_comb=256).reshape(b, l, d)
    kd_p = kd32.reshape(1, b, l, 2 * H_A, HD_A)
    vd_p = vd32.reshape(1, b, l, H_A, 2 * HD_A)
    km_p = km32.reshape(1, b, l, H_B, HD_B)
    vm_p = vm32.reshape(1, b, l, H_B, HD_B)

    pos_s = past_len + (jnp.arange(s * t, dtype=jnp.int32) % t)
    cos, slo, shi, half = _rope_tables(pos_s, HD_A)
    xs2 = x_sample.reshape(s * t, d)
    (kd32, vd32, km32, vm32, qd, kd, vd, qm, km, vm, ga, gb) = _proj(
        xs2, w_in16, cos, slo, shi, half, tm=s * t, with_means=False)
    r3 = lambda a: a.reshape(s, t, a.shape[-1])
    pages = 8
    tr = lambda cch: jnp.transpose(cch[lyr], (0, 2, 3, 1))
    oa = _diff_sample(page_table, lam_p, subln, _block_diag_queries(r3(qd)), _pad_rows(r3(kd), LANES),
                      _pad_rows(r3(vd), LANES), tr(cache_k_diff), cache_v_diff[lyr], lambda_init, pages=pages, n_new=t)
    ob = _moba_sample(page_table, _block_diag_queries(r3(qm)), _pad_rows(r3(km), LANES), _pad_rows(r3(vm), LANES),
                      tr(cache_k_moba), tr(cache_v_moba), pages=pages, n_new=t)
    oa = oa[:, :t].reshape(s * t, W_A)
    ob = ob[:, :t].reshape(s * t, W_B)
    y_s = _ffn(xs2, oa, ob, ga, gb, wts, alpha, tm_merge=s * t, tm_exp=256,
               tm_comb=min(256, s * t)).reshape(s, t, d)
    kd_s = kd32.reshape(1, s, t, 2 * H_A, HD_A)
    vd_s = vd32.reshape(1, s, t, H_A, 2 * HD_A)
    km_s = km32.reshape(1, s, t, H_B, HD_B)
    vm_s = vm32.reshape(1, s, t, H_B, HD_B)

    return (y_p, y_s, kd_p, vd_p, km_p, vm_p, kd_s, vd_s, km_s, vm_s)
```

```python
import functools
import math

import jax
import jax.numpy as jnp
import numpy as np
from jax import lax
from jax.experimental import pallas as pl
from jax.experimental.pallas import tpu as pltpu

F32 = jnp.float32
BF16 = jnp.bfloat16

LANES = 128
SUBLANES = 8

H_A = 4
HD_A = 64
H_B = 8
HD_B = 64
W_A = H_A * 2 * HD_A
W_B = H_B * HD_B
MOBA_BLOCK = 256
MOBA_TOPK = 3
ROPE_THETA = 500000.0
ROPE_FRACTION = 4
N_GROUPS = 4
EXPERTS_PER_GROUP = 8
N_EXPERTS = N_GROUPS * EXPERTS_PER_GROUP
TOPK_IN_GROUP = 2
LN_EPS = 1e-5
NEG = -1e30
SCALE_A = HD_A ** -0.5
SCALE_B = HD_B ** -0.5

VMEM_LIMIT = 56 * 1024 * 1024
SAMPLE_PAGES_PER_STEP = 8
Q_TILE = 512
KEY_CHUNK = 512
DIFF_ROW_SPLITS = 4
MOBA_ROW_SPLITS = 2


def _cparams(sem):
    return pltpu.CompilerParams(dimension_semantics=sem, vmem_limit_bytes=VMEM_LIMIT)


def _rope_tables(pos, head_dim):
    rot = head_dim // ROPE_FRACTION
    half = rot // 2
    inv_freq = ROPE_THETA ** (-jnp.arange(half, dtype=F32) * 2.0 / rot)
    ang = pos.astype(F32)[:, None] * inv_freq[None, :]
    cos, sin = jnp.cos(ang), jnp.sin(ang)
    d = jnp.arange(LANES) % head_dim
    f = d % half
    cos_l = jnp.where(d[None, :] < rot, cos[:, f], 1.0)
    sin_lo = jnp.where(d[None, :] < half, -sin[:, f], 0.0)
    sin_hi = jnp.where((d[None, :] >= half) & (d[None, :] < rot), sin[:, f], 0.0)
    return cos_l.astype(F32), sin_lo.astype(F32), sin_hi.astype(F32), half


def _proj_kernel(x_ref, w_ref, cos_ref, slo_ref, shi_ref,
                 kd32_ref, vd32_ref, km32_ref, vm32_ref,
                 qd_ref, kd_ref, vd_ref, qm_ref, km_ref, vm_ref, ga_ref, gb_ref, *rest,
                 half, d_model, with_means):
    xb = x_ref[...].astype(BF16)
    cos = cos_ref[...]
    slo = slo_ref[...]
    shi = shi_ref[...]

    def mm(c0, width):
        return jnp.dot(xb, w_ref[:, c0:c0 + width], preferred_element_type=F32)

    def rope(p):
        return p * cos + pltpu.roll(p, LANES - half, axis=1) * slo + pltpu.roll(p, half, axis=1) * shi

    def rope_store(c0, width, scale, out32_ref, out16_ref):
        p = mm(c0, width)
        for j in range(width // LANES):
            r = rope(p[:, j * LANES:(j + 1) * LANES])
            if out32_ref is not None:
                out32_ref[:, j * LANES:(j + 1) * LANES] = r
            out16_ref[:, j * LANES:(j + 1) * LANES] = (r * scale).astype(BF16)
        return None

    c = 0
    rope_store(c, W_A, SCALE_A, None, qd_ref); c += W_A
    rope_store(c, W_A, 1.0, kd32_ref, kd_ref); c += W_A
    p = mm(c, W_A); vd32_ref[...] = p; vd_ref[...] = p.astype(BF16); c += W_A
    rope_store(c, W_B, SCALE_B, None, qm_ref); c += W_B
    rope_store(c, W_B, 1.0, km32_ref, km_ref)
    if with_means:
        kmean_ref = rest[0]
        tm = x_ref.shape[0]
        for blk in range(tm // MOBA_BLOCK):
            kb = km32_ref[blk * MOBA_BLOCK:(blk + 1) * MOBA_BLOCK, :]
            kmean_ref[blk] = jnp.sum(kb, axis=0, keepdims=True) * (1.0 / MOBA_BLOCK)
    c += W_B
    p = mm(c, W_B); vm32_ref[...] = p; vm_ref[...] = p.astype(BF16); c += W_B
    for out_ref in (ga_ref, gb_ref):
        for j in range(d_model // 512):
            out_ref[:, j * 512:(j + 1) * 512] = jax.nn.sigmoid(mm(c, 512))
            c += 512


def _proj(x2d, w_in16, cos, slo, shi, half, *, tm, with_means):
    n, d_model = x2d.shape
    n_tab = cos.shape[0]
    assert n % tm == 0 and n_tab % tm == 0
    tab_blocks = n_tab // tm
    cols = w_in16.shape[1]
    row = lambda i: (i, 0)
    tab = lambda i: (i % tab_blocks, 0)
    out_shape = [jax.ShapeDtypeStruct((n, 512), F32)] * 4 + [jax.ShapeDtypeStruct((n, 512), BF16)] * 6 \
        + [jax.ShapeDtypeStruct((n, d_model), F32)] * 2
    out_specs = [pl.BlockSpec((tm, 512), row)] * 10 + [pl.BlockSpec((tm, d_model), row)] * 2
    if with_means:
        assert tm % MOBA_BLOCK == 0
        out_shape.append(jax.ShapeDtypeStruct((n // MOBA_BLOCK, 1, W_B), F32))
        out_specs.append(pl.BlockSpec((tm // MOBA_BLOCK, 1, W_B), lambda i: (i, 0, 0)))
    return pl.pallas_call(
        functools.partial(_proj_kernel, half=half, d_model=d_model, with_means=with_means),
        grid=(n // tm,),
        in_specs=[pl.BlockSpec((tm, d_model), row),
                  pl.BlockSpec((d_model, cols), lambda i: (0, 0)),
                  pl.BlockSpec((tm, LANES), tab), pl.BlockSpec((tm, LANES), tab), pl.BlockSpec((tm, LANES), tab)],
        out_specs=out_specs,
        out_shape=out_shape,
        compiler_params=_cparams(("parallel",)),
    )(x2d, w_in16, cos, slo, shi)


def _stack_halves(q):
    lane = lax.broadcasted_iota(jnp.int32, q.shape, 1)
    zero = jnp.zeros_like(q)
    return jnp.concatenate([jnp.where(lane < LANES // 2, q, zero), jnp.where(lane >= LANES // 2, q, zero)], axis=0)


def _flash_update(s, v, m_ref, l_ref, acc_ref):
    m_old = m_ref[...]
    m_new = jnp.maximum(m_old, jnp.max(s, axis=1, keepdims=True))
    alpha = jnp.exp(m_old - m_new)
    p = jnp.exp(s - m_new)
    l_ref[...] = alpha * l_ref[...] + jnp.sum(p, axis=1, keepdims=True)
    acc_ref[...] = alpha * acc_ref[...] + jnp.dot(p.astype(BF16), v, preferred_element_type=F32)
    m_ref[...] = m_new


def _causal_mask(tq, tk):
    row = lax.broadcasted_iota(jnp.int32, (2 * tq, tk), 0)
    col = lax.broadcasted_iota(jnp.int32, (2 * tq, tk), 1)
    row = jnp.where(row >= tq, row - tq, row)
    return col <= row


_NT = (((1,), (1,)), ((), ()))


def _diff_lambda(lam_ref, lambda_init):
    lp = lam_ref[...]
    return (jnp.exp(jnp.sum(lp[0:1] * lp[1:2], axis=1, keepdims=True))
            - jnp.exp(jnp.sum(lp[2:3] * lp[3:4], axis=1, keepdims=True)) + lambda_init)


def _diff_finish(o1, o2, lam, sub, lambda_init):
    o = o1 - lam * o2
    o = o * lax.rsqrt(jnp.mean(o * o, axis=1, keepdims=True) + LN_EPS)
    return o * sub * (1.0 - lambda_init)


def _flash_chunk(s, v_aug, m_ref, acc_ref):
    groups = s.shape[1] // LANES
    part = s[:, :LANES]
    for g in range(1, groups):
        part = jnp.maximum(part, s[:, g * LANES:(g + 1) * LANES])
    m_old = m_ref[...]
    m_new = jnp.maximum(m_old, jnp.max(part, axis=1, keepdims=True))
    alpha = jnp.exp(m_old - m_new)
    p = jnp.exp(s - jnp.concatenate([m_new] * groups, axis=1))
    acc_ref[...] = acc_ref[...] * jnp.concatenate([alpha, alpha], axis=1) \
        + jnp.dot(p.astype(BF16), v_aug, preferred_element_type=F32)
    m_ref[...] = m_new


def _attend_causal(q, key_chunk, value_chunk, m_ref, acc_ref, sbuf0, sbuf1, *, tq, ck, q_start, splits):
    rows = q.shape[0]
    rs = rows // splits
    assert tq % rs == 0
    n_full = q_start // ck

    def scores(start):
        k = key_chunk(start)
        return [lax.dot_general(q[i * rs:(i + 1) * rs], k, _NT, preferred_element_type=F32) for i in range(splits)]

    def consume(parts, start):
        v_aug = value_chunk(start)
        for i in range(splits):
            _flash_chunk(parts[i], v_aug, m_ref.at[pl.ds(i * rs, rs)], acc_ref.at[pl.ds(i * rs, rs)])

    def put(buf, parts):
        for i in range(splits):
            buf[pl.ds(i * rs, rs), :] = parts[i]

    def get(buf):
        return [buf[pl.ds(i * rs, rs), :] for i in range(splits)]

    def chunk_start(j):
        return pl.multiple_of(j * ck, ck)

    put(sbuf0, scores(0))
    tail = chunk_start(n_full)
    parts = scores(tail)
    for i in range(splits):
        row = lax.broadcasted_iota(jnp.int32, (rs, ck), 0) + (i * rs) % tq
        col = lax.broadcasted_iota(jnp.int32, (rs, ck), 1)
        parts[i] = jnp.where(col + tail <= row + q_start, parts[i], NEG)
    consume(parts, tail)

    def pair(jj, carry):
        j0 = 2 * jj
        put(sbuf1, scores(chunk_start(jnp.minimum(j0 + 1, n_full - 1))))
        consume(get(sbuf0), chunk_start(j0))

        @pl.when(j0 + 1 < n_full)
        def _():
            put(sbuf0, scores(chunk_start(jnp.minimum(j0 + 2, n_full - 1))))
            consume(get(sbuf1), chunk_start(j0 + 1))

        return carry

    lax.fori_loop(0, (n_full + 1) // 2, pair, 0)


def _diff_prompt_kernel(lam_ref, sub_ref, q_ref, k_ref, v_ref, o_ref, m_ref, acc_ref, sbuf0, sbuf1,
                        *, tq, ck, lambda_init):
    qi = pl.program_id(2)
    qs = _stack_halves(q_ref[0])
    m_ref[...] = jnp.full(m_ref.shape, NEG, F32)
    acc_ref[...] = jnp.zeros(acc_ref.shape, F32)
    ones = jnp.ones((ck, LANES), BF16)

    def key_chunk(start):
        return k_ref[0, pl.ds(start, ck), :]

    def value_chunk(start):
        return jnp.concatenate([v_ref[0, pl.ds(start, ck), :], ones], axis=1)

    _attend_causal(qs, key_chunk, value_chunk, m_ref, acc_ref, sbuf0, sbuf1,
                   tq=tq, ck=ck, q_start=qi * tq, splits=DIFF_ROW_SPLITS)

    acc = acc_ref[...]
    on = acc[:, :LANES] / acc[:, LANES:]
    lam = _diff_lambda(lam_ref, lambda_init)
    o_ref[0] = _diff_finish(on[:tq], on[tq:], lam, sub_ref[...], lambda_init).astype(o_ref.dtype)


def _diff_prompt(q, k, v, diff_lambda, subln, lambda_init, *, tq, ck):
    b, l, _ = q.shape
    assert l % ck == 0 and ck % tq == 0
    return pl.pallas_call(
        functools.partial(_diff_prompt_kernel, tq=tq, ck=ck, lambda_init=lambda_init),
        name="diff_prompt",
        grid=(b, H_A, l // tq),
        in_specs=[pl.BlockSpec((4, HD_A), lambda bi, h, qi: (0, 0)),
                  pl.BlockSpec((1, LANES), lambda bi, h, qi: (0, 0)),
                  pl.BlockSpec((1, tq, LANES), lambda bi, h, qi: (bi, qi, h)),
                  pl.BlockSpec((1, l, LANES), lambda bi, h, qi: (bi, 0, h)),
                  pl.BlockSpec((1, l, LANES), lambda bi, h, qi: (bi, 0, h))],
        out_specs=pl.BlockSpec((1, tq, LANES), lambda bi, h, qi: (bi, qi, h)),
        out_shape=jax.ShapeDtypeStruct((b, l, W_A), BF16),
        scratch_shapes=[pltpu.VMEM((2 * tq, LANES), F32), pltpu.VMEM((2 * tq, 2 * LANES), F32),
                        pltpu.VMEM((2 * tq, ck), F32), pltpu.VMEM((2 * tq, ck), F32)],
        compiler_params=_cparams(("parallel", "parallel", "arbitrary")),
    )(diff_lambda, subln, q, k, v)


def _moba_select(gate, n_past, own=None):
    blk = lax.broadcasted_iota(jnp.int32, gate.shape, 1).astype(F32)
    ninf = jnp.float32(-jnp.inf)
    g = jnp.where(blk < n_past, gate, ninf)
    bias = jnp.full(gate.shape, NEG, F32)
    if own is not None:
        bias = jnp.where(blk == own, 0.0, bias)
    for _ in range(MOBA_TOPK):
        mx = jnp.max(g, axis=1, keepdims=True)
        idx = jnp.min(jnp.where(g == mx, blk, float(LANES)), axis=1, keepdims=True)
        pick = (blk == idx) & (mx > ninf)
        bias = jnp.where(pick, 0.0, bias)
        g = jnp.where(pick, ninf, g)
    return bias


def _moba_prompt_kernel(q_ref, k_ref, v_ref, kmean_ref, o_ref, m_ref, acc_ref, sbuf0, sbuf1, *, ck):
    tq = q_ref.shape[1]
    qi = pl.program_id(2)
    qs = _stack_halves(q_ref[0])
    m_ref[...] = jnp.full(m_ref.shape, NEG, F32)
    acc_ref[...] = jnp.zeros(acc_ref.shape, F32)
    ones = jnp.ones((ck, LANES), BF16)

    gate = lax.dot_general(qs, kmean_ref[0].astype(BF16), _NT, preferred_element_type=F32)
    row = lax.broadcasted_iota(jnp.int32, gate.shape, 0)
    own = ((qi * tq + jnp.where(row >= tq, row - tq, row)) // MOBA_BLOCK).astype(F32)
    bias = _moba_select(gate, own, own=own)
    q_aug = jnp.concatenate([qs, bias.astype(BF16)], axis=1)
    lane = lax.broadcasted_iota(jnp.int32, (ck, LANES), 1)
    row_blk = lax.broadcasted_iota(jnp.int32, (ck, LANES), 0) // MOBA_BLOCK

    def key_chunk(start):
        onehot = jnp.where(lane == row_blk + start // MOBA_BLOCK, 1.0, 0.0).astype(BF16)
        return jnp.concatenate([k_ref[0, pl.ds(start, ck), :], onehot], axis=1)

    def value_chunk(start):
        return jnp.concatenate([v_ref[0, pl.ds(start, ck), :], ones], axis=1)

    _attend_causal(q_aug, key_chunk, value_chunk, m_ref, acc_ref, sbuf0, sbuf1,
                   tq=tq, ck=ck, q_start=qi * tq, splits=MOBA_ROW_SPLITS)

    acc = acc_ref[...]
    on = acc[:, :LANES] / acc[:, LANES:]
    lane_o = lax.broadcasted_iota(jnp.int32, (tq, LANES), 1)
    o_ref[0] = jnp.where(lane_o < LANES // 2, on[:tq], on[tq:]).astype(o_ref.dtype)


def _moba_prompt(q, k, v, kmean_pad, *, tq, ck):
    b, l, _ = q.shape
    assert l % ck == 0 and ck % tq == 0 and tq % MOBA_BLOCK == 0 and ck % MOBA_BLOCK == 0
    assert l // MOBA_BLOCK <= LANES
    return pl.pallas_call(
        functools.partial(_moba_prompt_kernel, ck=ck),
        name="moba_prompt",
        grid=(b, W_B // LANES, l // tq),
        in_specs=[pl.BlockSpec((1, tq, LANES), lambda bi, h, qi: (bi, qi, h)),
                  pl.BlockSpec((1, l, LANES), lambda bi, h, qi: (bi, 0, h)),
                  pl.BlockSpec((1, l, LANES), lambda bi, h, qi: (bi, 0, h)),
                  pl.BlockSpec((1, LANES, LANES), lambda bi, h, qi: (bi, 0, h))],
        out_specs=pl.BlockSpec((1, tq, LANES), lambda bi, h, qi: (bi, qi, h)),
        out_shape=jax.ShapeDtypeStruct((b, l, W_B), BF16),
        scratch_shapes=[pltpu.VMEM((2 * tq, LANES), F32), pltpu.VMEM((2 * tq, 2 * LANES), F32),
                        pltpu.VMEM((2 * tq, ck), F32), pltpu.VMEM((2 * tq, ck), F32)],
        compiler_params=_cparams(("parallel", "parallel", "arbitrary")),
    )(q, k, v, kmean_pad)


def _layer_norm(z, g, b):
    mu = jnp.mean(z, axis=1, keepdims=True)
    zc = z - mu
    var = jnp.mean(zc * zc, axis=1, keepdims=True)
    return zc * lax.rsqrt(var + LN_EPS) * g + b


def _first_argmax(vals, lane_f, valid):
    ninf = jnp.float32(-jnp.inf)
    v = jnp.where(valid, vals, ninf)
    mx = jnp.max(v, axis=1, keepdims=True)
    idx = jnp.min(jnp.where(v == mx, lane_f, float(LANES)), axis=1, keepdims=True)
    return mx, idx


def _route(logits):
    lane_i = lax.broadcasted_iota(jnp.int32, logits.shape, 1)
    lane = lane_i.astype(F32)
    is_grp = lane_i < N_GROUPS
    gmax, gidx = _first_argmax(logits, lane, is_grp)
    gex = jnp.where(is_grp, jnp.exp(logits - gmax), 0.0)
    gp = gex / jnp.sum(gex, axis=1, keepdims=True)
    g_p, g_idx = _first_argmax(gp, lane, is_grp)
    lo = N_GROUPS + g_idx * EXPERTS_PER_GROUP
    in_grp = (lane >= lo) & (lane < lo + EXPERTS_PER_GROUP)
    emax, _ = _first_argmax(logits, lane, in_grp)
    eex = jnp.where(in_grp, jnp.exp(logits - emax), 0.0)
    ep = eex / jnp.sum(eex, axis=1, keepdims=True)
    p1, i1 = _first_argmax(ep, lane, in_grp)
    p2, i2 = _first_argmax(ep, lane, in_grp & (lane != i1))
    den = p1 + p2
    e1 = g_idx * EXPERTS_PER_GROUP + (i1 - lo)
    e2 = g_idx * EXPERTS_PER_GROUP + (i2 - lo)
    out = jnp.where(lane_i == 0, e1, 0.0)
    out = jnp.where(lane_i == 1, e2, out)
    out = jnp.where(lane_i == 2, g_p * (p1 / den), out)
    out = jnp.where(lane_i == 3, g_p * (p2 / den), out)
    return out


def _merge_kernel(x_ref, oa_ref, ob_ref, ga_ref, gb_ref, wa_ref, wb_ref, wo_ref, g1_ref, b1_ref,
                  wr_ref, br_ref, h_ref, route_ref, *, alpha):
    br_a = jnp.dot(oa_ref[...], wa_ref[...], preferred_element_type=F32)
    br_b = jnp.dot(ob_ref[...], wb_ref[...], preferred_element_type=F32)
    t = ga_ref[...] * br_a + gb_ref[...] * br_b
    mix = jnp.dot(t.astype(BF16), wo_ref[...], preferred_element_type=F32)
    h = _layer_norm(alpha * x_ref[...] + mix, g1_ref[...], b1_ref[...])
    h_ref[...] = h
    logits = jnp.dot(h.astype(BF16), wr_ref[...], preferred_element_type=F32) + br_ref[...]
    route_ref[...] = _route(logits)


def _merge(x2d, oa, ob, ga, gb, wa16, wb16, wo16, g1, b1, wr16, br, alpha, *, tm):
    n, d = x2d.shape
    assert n % tm == 0
    row = lambda i: (i, 0)
    fix = lambda i: (0, 0)
    return pl.pallas_call(
        functools.partial(_merge_kernel, alpha=alpha),
        grid=(n // tm,),
        in_specs=[pl.BlockSpec((tm, d), row), pl.BlockSpec((tm, W_A), row), pl.BlockSpec((tm, W_B), row),
                  pl.BlockSpec((tm, d), row), pl.BlockSpec((tm, d), row),
                  pl.BlockSpec((W_A, d), fix), pl.BlockSpec((W_B, d), fix), pl.BlockSpec((d, d), fix),
                  pl.BlockSpec((1, d), fix), pl.BlockSpec((1, d), fix),
                  pl.BlockSpec((d, LANES), fix), pl.BlockSpec((1, LANES), fix)],
        out_specs=[pl.BlockSpec((tm, d), row), pl.BlockSpec((tm, LANES), row)],
        out_shape=[jax.ShapeDtypeStruct((n, d), F32), jax.ShapeDtypeStruct((n, LANES), F32)],
        compiler_params=_cparams(("parallel",)),
    )(x2d, oa, ob, ga, gb, wa16, wb16, wo16, g1, b1, wr16, br)


GATHER_UNROLL = 8


def _row_copy(idx_ref, r, src_hbm, dst_vmem, sem):
    return pltpu.make_async_copy(src_hbm.at[pl.ds(idx_ref[r], 1), :], dst_vmem.at[pl.ds(r, 1), :], sem)


def _gather_start(idx_ref, src_hbm, dst_vmem, sem):
    def body(g, c):
        for u in range(GATHER_UNROLL):
            _row_copy(idx_ref, g * GATHER_UNROLL + u, src_hbm, dst_vmem, sem).start()
        return c
    lax.fori_loop(0, dst_vmem.shape[0] // GATHER_UNROLL, body, 0)


def _gather_wait(idx_ref, src_hbm, dst_vmem, sem):
    def body(g, c):
        for u in range(GATHER_UNROLL):
            _row_copy(idx_ref, g * GATHER_UNROLL + u, src_hbm, dst_vmem, sem).wait()
        return c
    lax.fori_loop(0, dst_vmem.shape[0] // GATHER_UNROLL, body, 0)


def _expert_kernel(tile_e_ref, n_used_ref, src_ref, nxt_ref, h_hbm, wg_ref, wu_ref, wd_ref, out_ref, xbuf, sem):
    i = pl.program_id(0)
    n_used = n_used_ref[0]
    slot = i % 2

    @pl.when((i == 0) & (i < n_used))
    def _():
        _gather_start(src_ref, h_hbm, xbuf.at[0], sem.at[0])

    @pl.when(i + 1 < n_used)
    def _():
        _gather_start(nxt_ref, h_hbm, xbuf.at[1 - slot], sem.at[1 - slot])

    @pl.when(i < n_used)
    def _():
        _gather_wait(src_ref, h_hbm, xbuf.at[slot], sem.at[slot])
        xb = xbuf[slot].astype(BF16)
        a = jnp.dot(xb, wg_ref[0], preferred_element_type=F32)
        b = jnp.dot(xb, wu_ref[0], preferred_element_type=F32)
        act = (a * jax.nn.sigmoid(a)) * b
        out_ref[...] = jnp.dot(act.astype(BF16), wd_ref[0], preferred_element_type=F32)

    @pl.when(i >= n_used)
    def _():
        out_ref[...] = jnp.zeros(out_ref.shape, F32)


def _experts(h, src_tok, tile_expert, n_used, wg16, wu16, wd16, *, tm):
    n, d = h.shape
    r_max = src_tok.shape[0]
    de = wg16.shape[2]
    n_tiles = r_max // tm
    assert tm % GATHER_UNROLL == 0
    grid_spec = pltpu.PrefetchScalarGridSpec(
        num_scalar_prefetch=2,
        grid=(n_tiles,),
        in_specs=[pl.BlockSpec((tm,), lambda i, te, nu: (i,), memory_space=pltpu.SMEM),
                  pl.BlockSpec((tm,), lambda i, te, nu: (jnp.minimum(i + 1, n_tiles - 1),), memory_space=pltpu.SMEM),
                  pl.BlockSpec(memory_space=pl.ANY),
                  pl.BlockSpec((1, d, de), lambda i, te, nu: (te[i], 0, 0)),
                  pl.BlockSpec((1, d, de), lambda i, te, nu: (te[i], 0, 0)),
                  pl.BlockSpec((1, de, d), lambda i, te, nu: (te[i], 0, 0))],
        out_specs=pl.BlockSpec((tm, d), lambda i, te, nu: (i, 0)),
        scratch_shapes=[pltpu.VMEM((2, tm, d), F32), pltpu.SemaphoreType.DMA((2,))],
    )
    return pl.pallas_call(
        _expert_kernel,
        name="experts",
        grid_spec=grid_spec,
        out_shape=jax.ShapeDtypeStruct((r_max, d), F32),
        compiler_params=_cparams(("arbitrary",)),
    )(tile_expert, n_used, src_tok, src_tok, h, wg16, wu16, wd16)


def _combine_kernel(pos0_ref, pos1_ref, nxt0_ref, nxt1_ref, h_ref, w_ref, g2_ref, b2_ref, out_hbm, y_ref,
                    buf0, buf1, sem0, sem1, *, alpha):
    i = pl.program_id(0)
    slot = i % 2

    @pl.when(i == 0)
    def _():
        _gather_start(pos0_ref, out_hbm, buf0.at[0], sem0.at[0])
        _gather_start(pos1_ref, out_hbm, buf1.at[0], sem1.at[0])

    @pl.when(i + 1 < pl.num_programs(0))
    def _():
        _gather_start(nxt0_ref, out_hbm, buf0.at[1 - slot], sem0.at[1 - slot])
        _gather_start(nxt1_ref, out_hbm, buf1.at[1 - slot], sem1.at[1 - slot])

    _gather_wait(pos0_ref, out_hbm, buf0.at[slot], sem0.at[slot])
    _gather_wait(pos1_ref, out_hbm, buf1.at[slot], sem1.at[slot])
    w = w_ref[...]
    f = buf0[slot] * w[:, 2:3] + buf1[slot] * w[:, 3:4]
    y_ref[...] = _layer_norm(alpha * h_ref[...] + f, g2_ref[...], b2_ref[...])


def _combine(h, route, pos0, pos1, out_sorted, g2, b2, alpha, *, tm):
    n, d = h.shape
    assert n % tm == 0 and tm % GATHER_UNROLL == 0
    n_tiles = n // tm
    row = lambda i: (i, 0)
    fix = lambda i: (0, 0)
    cur = pl.BlockSpec((tm,), lambda i: (i,), memory_space=pltpu.SMEM)
    nxt = pl.BlockSpec((tm,), lambda i: (jnp.minimum(i + 1, n_tiles - 1),), memory_space=pltpu.SMEM)
    return pl.pallas_call(
        functools.partial(_combine_kernel, alpha=alpha),
        name="combine",
        grid=(n_tiles,),
        in_specs=[cur, cur, nxt, nxt,
                  pl.BlockSpec((tm, d), row), pl.BlockSpec((tm, LANES), row),
                  pl.BlockSpec((1, d), fix), pl.BlockSpec((1, d), fix),
                  pl.BlockSpec(memory_space=pl.ANY)],
        out_specs=pl.BlockSpec((tm, d), row),
        out_shape=jax.ShapeDtypeStruct((n, d), F32),
        scratch_shapes=[pltpu.VMEM((2, tm, d), F32), pltpu.VMEM((2, tm, d), F32),
                        pltpu.SemaphoreType.DMA((2,)), pltpu.SemaphoreType.DMA((2,))],
        compiler_params=_cparams(("arbitrary",)),
    )(pos0, pos1, pos0, pos1, h, route, g2, b2, out_sorted)


def _moe_plan(route, tm):
    n = route.shape[0]
    flat_e = route[:, :TOPK_IN_GROUP].astype(jnp.int32).reshape(-1)
    n_asg = flat_e.shape[0]
    order = jnp.argsort(flat_e, stable=True).astype(jnp.int32)
    rank = jnp.argsort(order).astype(jnp.int32)
    experts = jnp.arange(N_EXPERTS, dtype=jnp.int32)
    sizes = jnp.sum((flat_e[:, None] == experts[None, :]).astype(jnp.int32), axis=0)
    padded = ((sizes + tm - 1) // tm) * tm
    pend = jnp.cumsum(padded)
    pstart = pend - padded
    start = jnp.cumsum(sizes) - sizes
    pos = (pstart[flat_e] + (rank - start[flat_e])).reshape(n, TOPK_IN_GROUP)
    r_max = ((n_asg + N_EXPERTS * (tm - 1)) // tm + 1) * tm
    tile_start = jnp.arange(r_max // tm, dtype=jnp.int32) * tm
    tile_expert = jnp.minimum(jnp.sum((tile_start[:, None] >= pend[None, :]).astype(jnp.int32), axis=1),
                              N_EXPERTS - 1)
    n_used = (pend[-1] // tm).astype(jnp.int32).reshape(1)
    row = jnp.arange(r_max, dtype=jnp.int32)
    row_e = jnp.repeat(tile_expert, tm)
    off = row - pstart[row_e]
    slot = jnp.clip(start[row_e] + off, 0, n_asg - 1)
    src_tok = jnp.where(off < sizes[row_e], order[slot] // TOPK_IN_GROUP, 0).astype(jnp.int32)
    return src_tok, tile_expert, n_used, pos[:, 0], pos[:, 1]


ROWS_PER_HEAD = SUBLANES
N_SUB = 8
QROWS = N_SUB * ROWS_PER_HEAD


def _block_diag_queries(q):
    s, t, w = q.shape
    qp = jnp.pad(q, ((0, 0), (0, ROWS_PER_HEAD - t), (0, 0)))
    tiled = jnp.tile(qp, (1, N_SUB, 1))
    row_head = jnp.arange(QROWS)[:, None] // ROWS_PER_HEAD
    lane_head = jnp.arange(w)[None, :] // (w // N_SUB)
    return jnp.where((row_head == lane_head)[None], tiled, jnp.zeros_like(tiled))


def _new_token_mask(n_new):
    row = lax.broadcasted_iota(jnp.int32, (QROWS, LANES), 0) % ROWS_PER_HEAD
    col = lax.broadcasted_iota(jnp.int32, (QROWS, LANES), 1)
    return (col <= row) & (col < n_new)


def _diff_sample_kernel(pt_ref, lam_ref, sub_ref, q_ref, kn_ref, vn_ref, *refs, pages, n_new, lambda_init):
    k_refs = refs[:pages]
    v_refs = refs[pages:2 * pages]
    o_ref, m_ref, l_ref, acc_ref = refs[2 * pages:]
    c = pl.program_id(1)
    q = q_ref[0]

    @pl.when(c == 0)
    def _():
        m_ref[...] = jnp.full(m_ref.shape, NEG, F32)
        l_ref[...] = jnp.zeros(l_ref.shape, F32)
        acc_ref[...] = jnp.zeros(acc_ref.shape, F32)
        s = lax.dot_general(q, kn_ref[0], _NT, preferred_element_type=F32)
        s = jnp.where(_new_token_mask(n_new), s, NEG)
        _flash_update(s, vn_ref[0], m_ref, l_ref, acc_ref)

    kt = jnp.concatenate([k_refs[j][0].reshape(q.shape[1], -1).astype(BF16) for j in range(pages)], axis=1)
    s = jnp.dot(q, kt, preferred_element_type=F32)
    v = jnp.concatenate(
        [jnp.concatenate([v_refs[j][0, :, h, :] for h in range(H_A)], axis=1).astype(BF16) for j in range(pages)],
        axis=0)
    _flash_update(s, v, m_ref, l_ref, acc_ref)

    @pl.when(c == pl.num_programs(1) - 1)
    def _():
        on = acc_ref[...] / l_ref[...]
        lam = _diff_lambda(lam_ref, lambda_init)
        for h in range(H_A):
            o1 = on[(2 * h) * ROWS_PER_HEAD:(2 * h + 1) * ROWS_PER_HEAD, h * LANES:(h + 1) * LANES]
            o2 = on[(2 * h + 1) * ROWS_PER_HEAD:(2 * h + 2) * ROWS_PER_HEAD, h * LANES:(h + 1) * LANES]
            o_ref[0, :, h * LANES:(h + 1) * LANES] = _diff_finish(o1, o2, lam, sub_ref[...], lambda_init).astype(o_ref.dtype)


def _page_specs(pages, page_shape, n_chunks, phase):
    def make(j):
        def index_map(b, c, pt):
            if phase == 0:
                c = jnp.minimum(c, n_chunks - 1)
            elif phase == 1:
                c = jnp.maximum(c - n_chunks, 0)
            return (pt[b, c * pages + j],) + (0,) * len(page_shape)
        return pl.BlockSpec((1,) + tuple(page_shape), index_map)
    return [make(j) for j in range(pages)]


def _diff_sample(page_table, diff_lambda, subln, qbd, k_new, v_new, cache_kt, cache_v, lambda_init, *, pages, n_new):
    s, n_pages = page_table.shape
    width = qbd.shape[2]
    page = cache_kt.shape[3]
    assert n_pages % pages == 0
    n_chunks = n_pages // pages
    per_seq = lambda b, c, pt: (b, 0, 0)
    fix = lambda b, c, pt: (0, 0)
    grid_spec = pltpu.PrefetchScalarGridSpec(
        num_scalar_prefetch=1,
        grid=(s, n_chunks),
        in_specs=[pl.BlockSpec((4, HD_A), fix), pl.BlockSpec((1, LANES), fix),
                  pl.BlockSpec((1, QROWS, width), per_seq),
                  pl.BlockSpec((1, LANES, width), per_seq), pl.BlockSpec((1, LANES, width), per_seq)]
        + _page_specs(pages, cache_kt.shape[1:], n_chunks, None) + _page_specs(pages, cache_v.shape[1:], n_chunks, None),
        out_specs=pl.BlockSpec((1, ROWS_PER_HEAD, width), per_seq),
        scratch_shapes=[pltpu.VMEM((QROWS, 1), F32), pltpu.VMEM((QROWS, 1), F32), pltpu.VMEM((QROWS, width), F32)],
    )
    return pl.pallas_call(
        functools.partial(_diff_sample_kernel, pages=pages, n_new=n_new, lambda_init=lambda_init),
        name="diff_sample",
        grid_spec=grid_spec,
        out_shape=jax.ShapeDtypeStruct((s, ROWS_PER_HEAD, width), BF16),
        compiler_params=_cparams(("parallel", "arbitrary")),
    )(page_table, diff_lambda, subln, qbd, k_new, v_new, *([cache_kt] * pages), *([cache_v] * pages))


def _moba_sample_kernel(pt_ref, q_ref, kn_ref, vn_ref, *refs, pages, page, n_chunks, n_new):
    k_refs = refs[:pages]
    v_refs = refs[pages:2 * pages]
    o_ref, s_ref, p_ref, gate_ref, l_ref, acc_ref = refs[2 * pages:]
    c = pl.program_id(1)
    q = q_ref[0]
    span = pages * page
    blocks_per_page = MOBA_BLOCK // page
    lane = lax.broadcasted_iota(jnp.int32, (QROWS, LANES), 1)

    @pl.when(c == 0)
    def _():
        gate_ref[...] = jnp.zeros(gate_ref.shape, F32)

    @pl.when(c < n_chunks)
    def _():
        g = gate_ref[...]
        kt = jnp.concatenate([k_refs[j][0].reshape(q.shape[1], -1).astype(BF16) for j in range(pages)], axis=1)
        s_chunk = jnp.dot(q, kt, preferred_element_type=F32)
        s_ref[:, pl.ds(pl.multiple_of(c * span, span), span)] = s_chunk
        for j in range(pages):
            sj = s_chunk[:, j * page:(j + 1) * page]
            blk = (c * pages + j) // blocks_per_page
            g = g + jnp.where(lane == blk, jnp.sum(sj, axis=1, keepdims=True) * (1.0 / MOBA_BLOCK), 0.0)
        gate_ref[...] = g

    @pl.when(c == n_chunks - 1)
    def _():
        n_past = (n_chunks * span) // MOBA_BLOCK
        bias = _moba_select(gate_ref[...], float(n_past))
        s_new = lax.dot_general(q, kn_ref[0], _NT, preferred_element_type=F32)
        s_new = jnp.where(_new_token_mask(n_new), s_new, NEG)
        m = jnp.max(s_new, axis=1, keepdims=True)
        bias16 = bias.astype(BF16)
        brow = lax.broadcasted_iota(jnp.int32, (LANES, span), 0)
        bcol = lax.broadcasted_iota(jnp.int32, (LANES, span), 1)
        for cc in range(n_chunks):
            expand = jnp.where((bcol + cc * span) // MOBA_BLOCK == brow, 1.0, 0.0).astype(BF16)
            sc = s_ref[:, cc * span:(cc + 1) * span] + jnp.dot(bias16, expand, preferred_element_type=F32)
            s_ref[:, cc * span:(cc + 1) * span] = sc
            m = jnp.maximum(m, jnp.max(sc, axis=1, keepdims=True))
        p_new = jnp.exp(s_new - m)
        l = jnp.sum(p_new, axis=1, keepdims=True)
        for cc in range(n_chunks):
            pc = jnp.exp(s_ref[:, cc * span:(cc + 1) * span] - m)
            l = l + jnp.sum(pc, axis=1, keepdims=True)
            p_ref[:, cc * span:(cc + 1) * span] = pc.astype(BF16)
        l_ref[...] = l
        acc_ref[...] = jnp.dot(p_new.astype(BF16), vn_ref[0], preferred_element_type=F32)

    @pl.when(c >= n_chunks)
    def _():
        cv = c - n_chunks
        acc = acc_ref[...]
        for j in range(pages):
            pj = p_ref[:, pl.ds(pl.multiple_of(cv * span + j * page, page), page)]
            vt = v_refs[j][0].reshape(acc.shape[1], -1).astype(BF16)
            acc = acc + lax.dot_general(pj, vt, _NT, preferred_element_type=F32)
        acc_ref[...] = acc

    @pl.when(c == 2 * n_chunks - 1)
    def _():
        on = acc_ref[...] / l_ref[...]
        lane_o = lax.broadcasted_iota(jnp.int32, (ROWS_PER_HEAD, on.shape[1]), 1) // HD_B
        out = jnp.zeros((ROWS_PER_HEAD, on.shape[1]), F32)
        for h in range(H_B):
            out = jnp.where(lane_o == h, on[h * ROWS_PER_HEAD:(h + 1) * ROWS_PER_HEAD, :], out)
        o_ref[0] = out.astype(o_ref.dtype)


def _moba_sample(page_table, qbd, k_new, v_new, cache_kt, cache_vt, *, pages, n_new):
    s, n_pages = page_table.shape
    width = qbd.shape[2]
    page = cache_kt.shape[3]
    assert n_pages % pages == 0 and MOBA_BLOCK % page == 0 and (n_pages * page) % MOBA_BLOCK == 0
    assert (n_pages * page) // MOBA_BLOCK <= LANES
    n_chunks = n_pages // pages
    past = n_pages * page
    per_seq = lambda b, c, pt: (b, 0, 0)
    grid_spec = pltpu.PrefetchScalarGridSpec(
        num_scalar_prefetch=1,
        grid=(s, 2 * n_chunks),
        in_specs=[pl.BlockSpec((1, QROWS, width), per_seq),
                  pl.BlockSpec((1, LANES, width), per_seq), pl.BlockSpec((1, LANES, width), per_seq)]
        + _page_specs(pages, cache_kt.shape[1:], n_chunks, 0) + _page_specs(pages, cache_vt.shape[1:], n_chunks, 1),
        out_specs=pl.BlockSpec((1, ROWS_PER_HEAD, width), per_seq),
        scratch_shapes=[pltpu.VMEM((QROWS, past), F32), pltpu.VMEM((QROWS, past), BF16),
                        pltpu.VMEM((QROWS, LANES), F32), pltpu.VMEM((QROWS, 1), F32),
                        pltpu.VMEM((QROWS, width), F32)],
    )
    return pl.pallas_call(
        functools.partial(_moba_sample_kernel, pages=pages, page=page, n_chunks=n_chunks, n_new=n_new),
        name="moba_sample",
        grid_spec=grid_spec,
        out_shape=jax.ShapeDtypeStruct((s, ROWS_PER_HEAD, width), BF16),
        compiler_params=_cparams(("parallel", "arbitrary")),
    )(page_table, qbd, k_new, v_new, *([cache_kt] * pages), *([cache_vt] * pages))


def _pad_rows(a, rows):
    return jnp.pad(a, ((0, 0), (0, rows - a.shape[1]), (0, 0)))


def _ffn(x2d, oa, ob, ga, gb, wts, alpha, *, tm_merge, tm_exp, tm_comb):
    h, route = _merge(x2d, oa, ob, ga, gb, wts['wa'], wts['wb'], wts['wo'], wts['g1'], wts['b1'],
                      wts['wr'], wts['br'], alpha, tm=tm_merge)
    src_tok, tile_expert, n_used, pos0, pos1 = _moe_plan(route, tm_exp)
    out_sorted = _experts(h, src_tok, tile_expert, n_used, wts['wg'], wts['wu'], wts['wd'], tm=tm_exp)
    return _combine(h, route, pos0, pos1, out_sorted, wts['g2'], wts['b2'], alpha, tm=tm_comb)


def kernel(x_prompt, x_sample, cache_k_diff, cache_v_diff, cache_k_moba, cache_v_moba, page_table,
           w_in, w_br_a, w_br_b, w_out, diff_lambda, diff_subln, ln1_g, ln1_b,
           w_grp, b_grp, w_exp, b_exp, w_gate, w_up, w_down, ln2_g, ln2_b):
    depth = w_in.shape[0]
    assert depth == 1, "kernel written for a single layer"
    b, l, d = x_prompt.shape
    s, t, _ = x_sample.shape
    n_pool, page = cache_k_diff.shape[1], cache_k_diff.shape[2]
    past_len = page_table.shape[1] * page
    alpha = (2.0 * depth) ** 0.25
    lyr = 0
    lambda_init = 0.8 - 0.6 * math.exp(-0.3 * lyr)

    w_in16 = w_in[lyr].astype(BF16)
    n_route = N_GROUPS + N_EXPERTS
    wr = jnp.concatenate([w_grp[lyr], w_exp[lyr], jnp.zeros((d, LANES - n_route), F32)], axis=1).astype(BF16)
    br = jnp.concatenate([b_grp[lyr], b_exp[lyr], jnp.zeros((LANES - n_route,), F32)]).reshape(1, LANES)
    wts = dict(wa=w_br_a[lyr].astype(BF16), wb=w_br_b[lyr].astype(BF16), wo=w_out[lyr].astype(BF16),
               g1=ln1_g[lyr].reshape(1, d), b1=ln1_b[lyr].reshape(1, d), wr=wr, br=br,
               wg=w_gate[lyr].astype(BF16), wu=w_up[lyr].astype(BF16), wd=w_down[lyr].astype(BF16),
               g2=ln2_g[lyr].reshape(1, d), b2=ln2_b[lyr].reshape(1, d))
    lam_p = diff_lambda[lyr]
    subln = diff_subln[lyr].reshape(1, 2 * HD_A)

    cos, slo, shi, half = _rope_tables(jnp.arange(l, dtype=jnp.int32), HD_A)
    tm = min(512, l)
    xp2 = x_prompt.reshape(b * l, d)
    (kd32, vd32, km32, vm32, qd, kd, vd, qm, km, vm, ga, gb, kmean) = _proj(
        xp2, w_in16, cos, slo, shi, half, tm=tm, with_means=True)
    r3 = lambda a: a.reshape(b, l, a.shape[-1])
    ck = min(KEY_CHUNK, l)
    tq = min(Q_TILE, ck)
    oa = _diff_prompt(r3(qd), r3(kd), r3(vd), lam_p, subln, lambda_init, tq=tq, ck=ck)
    nb = l // MOBA_BLOCK
    kmean_pad = jnp.pad(kmean.reshape(b, nb, W_B), ((0, 0), (0, LANES - nb), (0, 0)))
    ob = _moba_prompt(r3(qm), r3(km), r3(vm), kmean_pad, tq=tq, ck=ck)
    y_p = _ffn(xp2, oa.reshape(b * l, W_A), ob.reshape(b * l, W_B), ga, gb, wts, alpha,
               tm_merge=tm, tm_exp=256, tm_comb=256).reshape(b, l, d)
    kd_p = kd32.reshape(1, b, l, 2 * H_A, HD_A)
    vd_p = vd32.reshape(1, b, l, H_A, 2 * HD_A)
    km_p = km32.reshape(1, b, l, H_B, HD_B)
    vm_p = vm32.reshape(1, b, l, H_B, HD_B)

    pos_s = past_len + (jnp.arange(s * t, dtype=jnp.int32) % t)
    cos, slo, shi, half = _rope_tables(pos_s, HD_A)
    xs2 = x_sample.reshape(s * t, d)
    (kd32, vd32, km32, vm32, qd, kd, vd, qm, km, vm, ga, gb) = _proj(
        xs2, w_in16, cos, slo, shi, half, tm=s * t, with_means=False)
    r3 = lambda a: a.reshape(s, t, a.shape[-1])
    pages = math.gcd(SAMPLE_PAGES_PER_STEP, page_table.shape[1])
    tr = lambda cch: jnp.transpose(cch[lyr], (0, 2, 3, 1))
    oa = _diff_sample(page_table, lam_p, subln, _block_diag_queries(r3(qd)), _pad_rows(r3(kd), LANES),
                      _pad_rows(r3(vd), LANES), tr(cache_k_diff), cache_v_diff[lyr], lambda_init, pages=pages, n_new=t)
    ob = _moba_sample(page_table, _block_diag_queries(r3(qm)), _pad_rows(r3(km), LANES), _pad_rows(r3(vm), LANES),
                      tr(cache_k_moba), tr(cache_v_moba), pages=pages, n_new=t)
    oa = oa[:, :t].reshape(s * t, W_A)
    ob = ob[:, :t].reshape(s * t, W_B)
    y_s = _ffn(xs2, oa, ob, ga, gb, wts, alpha, tm_merge=s * t, tm_exp=128,
               tm_comb=min(256, s * t)).reshape(s, t, d)
    kd_s = kd32.reshape(1, s, t, 2 * H_A, HD_A)
    vd_s = vd32.reshape(1, s, t, H_A, 2 * HD_A)
    km_s = km32.reshape(1, s, t, H_B, HD_B)
    vm_s = vm32.reshape(1, s, t, H_B, HD_B)

    return (y_p, y_s, kd_p, vd_p, km_p, vm_p, kd_s, vd_s, km_s, vm_s)
```

```python
import functools
import math

import jax
import jax.numpy as jnp
import numpy as np
from jax import lax
from jax.experimental import pallas as pl
from jax.experimental.pallas import tpu as pltpu

F32 = jnp.float32
BF16 = jnp.bfloat16

LANES = 128
SUBLANES = 8

H_A = 4
HD_A = 64
H_B = 8
HD_B = 64
W_A = H_A * 2 * HD_A
W_B = H_B * HD_B
MOBA_BLOCK = 256
MOBA_TOPK = 3
ROPE_THETA = 500000.0
ROPE_FRACTION = 4
N_GROUPS = 4
EXPERTS_PER_GROUP = 8
N_EXPERTS = N_GROUPS * EXPERTS_PER_GROUP
TOPK_IN_GROUP = 2
LN_EPS = 1e-5
NEG = -1e30
SCALE_A = HD_A ** -0.5
SCALE_B = HD_B ** -0.5

VMEM_LIMIT = 56 * 1024 * 1024
SAMPLE_PAGES_PER_STEP = 8
Q_TILE = 512
KEY_CHUNK = 512
DIFF_ROW_SPLITS = 4
MOBA_ROW_SPLITS = 2


def _cparams(sem):
    return pltpu.CompilerParams(dimension_semantics=sem, vmem_limit_bytes=VMEM_LIMIT)


def _rope_tables(pos, head_dim):
    rot = head_dim // ROPE_FRACTION
    half = rot // 2
    inv_freq = ROPE_THETA ** (-jnp.arange(half, dtype=F32) * 2.0 / rot)
    ang = pos.astype(F32)[:, None] * inv_freq[None, :]
    cos, sin = jnp.cos(ang), jnp.sin(ang)
    d = jnp.arange(LANES) % head_dim
    f = d % half
    cos_l = jnp.where(d[None, :] < rot, cos[:, f], 1.0)
    sin_lo = jnp.where(d[None, :] < half, -sin[:, f], 0.0)
    sin_hi = jnp.where((d[None, :] >= half) & (d[None, :] < rot), sin[:, f], 0.0)
    return cos_l.astype(F32), sin_lo.astype(F32), sin_hi.astype(F32), half


def _proj_kernel(x_ref, w_ref, cos_ref, slo_ref, shi_ref,
                 kd32_ref, vd32_ref, km32_ref, vm32_ref,
                 qd_ref, kd_ref, vd_ref, qm_ref, km_ref, vm_ref, ga_ref, gb_ref, *rest,
                 half, d_model, with_means):
    xb = x_ref[...].astype(BF16)
    cos = cos_ref[...]
    slo = slo_ref[...]
    shi = shi_ref[...]

    def mm(c0, width):
        return jnp.dot(xb, w_ref[:, c0:c0 + width], preferred_element_type=F32)

    def rope(p):
        return p * cos + pltpu.roll(p, LANES - half, axis=1) * slo + pltpu.roll(p, half, axis=1) * shi

    def rope_store(c0, width, scale, out32_ref, out16_ref):
        p = mm(c0, width)
        for j in range(width // LANES):
            r = rope(p[:, j * LANES:(j + 1) * LANES])
            if out32_ref is not None:
                out32_ref[:, j * LANES:(j + 1) * LANES] = r
            out16_ref[:, j * LANES:(j + 1) * LANES] = (r * scale).astype(BF16)
        return None

    c = 0
    rope_store(c, W_A, SCALE_A, None, qd_ref); c += W_A
    rope_store(c, W_A, 1.0, kd32_ref, kd_ref); c += W_A
    p = mm(c, W_A); vd32_ref[...] = p; vd_ref[...] = p.astype(BF16); c += W_A
    rope_store(c, W_B, SCALE_B, None, qm_ref); c += W_B
    rope_store(c, W_B, 1.0, km32_ref, km_ref)
    if with_means:
        kmean_ref = rest[0]
        tm = x_ref.shape[0]
        for blk in range(tm // MOBA_BLOCK):
            kb = km32_ref[blk * MOBA_BLOCK:(blk + 1) * MOBA_BLOCK, :]
            kmean_ref[blk] = jnp.sum(kb, axis=0, keepdims=True) * (1.0 / MOBA_BLOCK)
    c += W_B
    p = mm(c, W_B); vm32_ref[...] = p; vm_ref[...] = p.astype(BF16); c += W_B
    for out_ref in (ga_ref, gb_ref):
        for j in range(d_model // 512):
            out_ref[:, j * 512:(j + 1) * 512] = jax.nn.sigmoid(mm(c, 512))
            c += 512


def _proj(x2d, w_in16, cos, slo, shi, half, *, tm, with_means):
    n, d_model = x2d.shape
    n_tab = cos.shape[0]
    assert n % tm == 0 and n_tab % tm == 0
    tab_blocks = n_tab // tm
    cols = w_in16.shape[1]
    row = lambda i: (i, 0)
    tab = lambda i: (i % tab_blocks, 0)
    out_shape = [jax.ShapeDtypeStruct((n, 512), F32)] * 4 + [jax.ShapeDtypeStruct((n, 512), BF16)] * 6 \
        + [jax.ShapeDtypeStruct((n, d_model), F32)] * 2
    out_specs = [pl.BlockSpec((tm, 512), row)] * 10 + [pl.BlockSpec((tm, d_model), row)] * 2
    if with_means:
        assert tm % MOBA_BLOCK == 0
        out_shape.append(jax.ShapeDtypeStruct((n // MOBA_BLOCK, 1, W_B), F32))
        out_specs.append(pl.BlockSpec((tm // MOBA_BLOCK, 1, W_B), lambda i: (i, 0, 0)))
    return pl.pallas_call(
        functools.partial(_proj_kernel, half=half, d_model=d_model, with_means=with_means),
        grid=(n // tm,),
        in_specs=[pl.BlockSpec((tm, d_model), row),
                  pl.BlockSpec((d_model, cols), lambda i: (0, 0)),
                  pl.BlockSpec((tm, LANES), tab), pl.BlockSpec((tm, LANES), tab), pl.BlockSpec((tm, LANES), tab)],
        out_specs=out_specs,
        out_shape=out_shape,
        compiler_params=_cparams(("parallel",)),
    )(x2d, w_in16, cos, slo, shi)


def _stack_halves(q):
    lane = lax.broadcasted_iota(jnp.int32, q.shape, 1)
    zero = jnp.zeros_like(q)
    return jnp.concatenate([jnp.where(lane < LANES // 2, q, zero), jnp.where(lane >= LANES // 2, q, zero)], axis=0)


def _flash_update(s, v, m_ref, l_ref, acc_ref):
    m_old = m_ref[...]
    m_new = jnp.maximum(m_old, jnp.max(s, axis=1, keepdims=True))
    alpha = jnp.exp(m_old - m_new)
    p = jnp.exp(s - m_new)
    l_ref[...] = alpha * l_ref[...] + jnp.sum(p, axis=1, keepdims=True)
    acc_ref[...] = alpha * acc_ref[...] + jnp.dot(p.astype(BF16), v, preferred_element_type=F32)
    m_ref[...] = m_new


def _causal_mask(tq, tk):
    row = lax.broadcasted_iota(jnp.int32, (2 * tq, tk), 0)
    col = lax.broadcasted_iota(jnp.int32, (2 * tq, tk), 1)
    row = jnp.where(row >= tq, row - tq, row)
    return col <= row


_NT = (((1,), (1,)), ((), ()))


def _diff_lambda(lam_ref, lambda_init):
    lp = lam_ref[...]
    return (jnp.exp(jnp.sum(lp[0:1] * lp[1:2], axis=1, keepdims=True))
            - jnp.exp(jnp.sum(lp[2:3] * lp[3:4], axis=1, keepdims=True)) + lambda_init)


def _diff_finish(o1, o2, lam, sub, lambda_init):
    o = o1 - lam * o2
    o = o * lax.rsqrt(jnp.mean(o * o, axis=1, keepdims=True) + LN_EPS)
    return o * sub * (1.0 - lambda_init)


def _flash_chunk(s, v_aug, m_ref, acc_ref):
    groups = s.shape[1] // LANES
    part = s[:, :LANES]
    for g in range(1, groups):
        part = jnp.maximum(part, s[:, g * LANES:(g + 1) * LANES])
    m_old = m_ref[...]
    m_new = jnp.maximum(m_old, jnp.max(part, axis=1, keepdims=True))
    alpha = jnp.exp(m_old - m_new)
    p = jnp.exp(s - jnp.concatenate([m_new] * groups, axis=1))
    acc_ref[...] = acc_ref[...] * jnp.concatenate([alpha, alpha], axis=1) \
        + jnp.dot(p.astype(BF16), v_aug, preferred_element_type=F32)
    m_ref[...] = m_new


def _attend_causal(q, key_chunk, value_chunk, m_ref, acc_ref, sbuf0, sbuf1, *, tq, ck, q_start, splits):
    rows = q.shape[0]
    rs = rows // splits
    assert tq % rs == 0
    n_full = q_start // ck

    def scores(start):
        k = key_chunk(start)
        return [lax.dot_general(q[i * rs:(i + 1) * rs], k, _NT, preferred_element_type=F32) for i in range(splits)]

    def consume(parts, start):
        v_aug = value_chunk(start)
        for i in range(splits):
            _flash_chunk(parts[i], v_aug, m_ref.at[pl.ds(i * rs, rs)], acc_ref.at[pl.ds(i * rs, rs)])

    def put(buf, parts):
        for i in range(splits):
            buf[pl.ds(i * rs, rs), :] = parts[i]

    def get(buf):
        return [buf[pl.ds(i * rs, rs), :] for i in range(splits)]

    def chunk_start(j):
        return pl.multiple_of(j * ck, ck)

    put(sbuf0, scores(0))
    tail = chunk_start(n_full)
    parts = scores(tail)
    for i in range(splits):
        row = lax.broadcasted_iota(jnp.int32, (rs, ck), 0) + (i * rs) % tq
        col = lax.broadcasted_iota(jnp.int32, (rs, ck), 1)
        parts[i] = jnp.where(col + tail <= row + q_start, parts[i], NEG)
    consume(parts, tail)

    def pair(jj, carry):
        j0 = 2 * jj
        put(sbuf1, scores(chunk_start(jnp.minimum(j0 + 1, n_full - 1))))
        consume(get(sbuf0), chunk_start(j0))

        @pl.when(j0 + 1 < n_full)
        def _():
            put(sbuf0, scores(chunk_start(jnp.minimum(j0 + 2, n_full - 1))))
            consume(get(sbuf1), chunk_start(j0 + 1))

        return carry

    lax.fori_loop(0, (n_full + 1) // 2, pair, 0)


def _diff_prompt_kernel(lam_ref, sub_ref, q_ref, k_ref, v_ref, o_ref, m_ref, acc_ref, sbuf0, sbuf1,
                        *, tq, ck, lambda_init):
    qi = pl.program_id(2)
    qs = _stack_halves(q_ref[0])
    m_ref[...] = jnp.full(m_ref.shape, NEG, F32)
    acc_ref[...] = jnp.zeros(acc_ref.shape, F32)
    ones = jnp.ones((ck, LANES), BF16)

    def key_chunk(start):
        return k_ref[0, pl.ds(start, ck), :]

    def value_chunk(start):
        return jnp.concatenate([v_ref[0, pl.ds(start, ck), :], ones], axis=1)

    _attend_causal(qs, key_chunk, value_chunk, m_ref, acc_ref, sbuf0, sbuf1,
                   tq=tq, ck=ck, q_start=qi * tq, splits=DIFF_ROW_SPLITS)

    acc = acc_ref[...]
    on = acc[:, :LANES] / acc[:, LANES:]
    lam = _diff_lambda(lam_ref, lambda_init)
    o_ref[0] = _diff_finish(on[:tq], on[tq:], lam, sub_ref[...], lambda_init).astype(o_ref.dtype)


def _diff_prompt(q, k, v, diff_lambda, subln, lambda_init, *, tq, ck):
    b, l, _ = q.shape
    assert l % ck == 0 and ck % tq == 0
    return pl.pallas_call(
        functools.partial(_diff_prompt_kernel, tq=tq, ck=ck, lambda_init=lambda_init),
        name="diff_prompt",
        grid=(b, H_A, l // tq),
        in_specs=[pl.BlockSpec((4, HD_A), lambda bi, h, qi: (0, 0)),
                  pl.BlockSpec((1, LANES), lambda bi, h, qi: (0, 0)),
                  pl.BlockSpec((1, tq, LANES), lambda bi, h, qi: (bi, qi, h)),
                  pl.BlockSpec((1, l, LANES), lambda bi, h, qi: (bi, 0, h)),
                  pl.BlockSpec((1, l, LANES), lambda bi, h, qi: (bi, 0, h))],
        out_specs=pl.BlockSpec((1, tq, LANES), lambda bi, h, qi: (bi, qi, h)),
        out_shape=jax.ShapeDtypeStruct((b, l, W_A), BF16),
        scratch_shapes=[pltpu.VMEM((2 * tq, LANES), F32), pltpu.VMEM((2 * tq, 2 * LANES), F32),
                        pltpu.VMEM((2 * tq, ck), F32), pltpu.VMEM((2 * tq, ck), F32)],
        compiler_params=_cparams(("parallel", "parallel", "arbitrary")),
    )(diff_lambda, subln, q, k, v)


def _moba_select(gate, n_past, own=None):
    blk = lax.broadcasted_iota(jnp.int32, gate.shape, 1).astype(F32)
    ninf = jnp.float32(-jnp.inf)
    g = jnp.where(blk < n_past, gate, ninf)
    bias = jnp.full(gate.shape, NEG, F32)
    if own is not None:
        bias = jnp.where(blk == own, 0.0, bias)
    for _ in range(MOBA_TOPK):
        mx = jnp.max(g, axis=1, keepdims=True)
        idx = jnp.min(jnp.where(g == mx, blk, float(LANES)), axis=1, keepdims=True)
        pick = (blk == idx) & (mx > ninf)
        bias = jnp.where(pick, 0.0, bias)
        g = jnp.where(pick, ninf, g)
    return bias


def _moba_prompt_kernel(q_ref, k_ref, v_ref, kmean_ref, o_ref, m_ref, acc_ref, sbuf0, sbuf1, *, ck):
    tq = q_ref.shape[1]
    qi = pl.program_id(2)
    qs = _stack_halves(q_ref[0])
    m_ref[...] = jnp.full(m_ref.shape, NEG, F32)
    acc_ref[...] = jnp.zeros(acc_ref.shape, F32)
    ones = jnp.ones((ck, LANES), BF16)

    gate = lax.dot_general(qs, kmean_ref[0].astype(BF16), _NT, preferred_element_type=F32)
    row = lax.broadcasted_iota(jnp.int32, gate.shape, 0)
    own = ((qi * tq + jnp.where(row >= tq, row - tq, row)) // MOBA_BLOCK).astype(F32)
    bias = _moba_select(gate, own, own=own)
    q_aug = jnp.concatenate([qs, bias.astype(BF16)], axis=1)
    lane = lax.broadcasted_iota(jnp.int32, (ck, LANES), 1)
    row_blk = lax.broadcasted_iota(jnp.int32, (ck, LANES), 0) // MOBA_BLOCK

    def key_chunk(start):
        onehot = jnp.where(lane == row_blk + start // MOBA_BLOCK, 1.0, 0.0).astype(BF16)
        return jnp.concatenate([k_ref[0, pl.ds(start, ck), :], onehot], axis=1)

    def value_chunk(start):
        return jnp.concatenate([v_ref[0, pl.ds(start, ck), :], ones], axis=1)

    _attend_causal(q_aug, key_chunk, value_chunk, m_ref, acc_ref, sbuf0, sbuf1,
                   tq=tq, ck=ck, q_start=qi * tq, splits=MOBA_ROW_SPLITS)

    acc = acc_ref[...]
    on = acc[:, :LANES] / acc[:, LANES:]
    lane_o = lax.broadcasted_iota(jnp.int32, (tq, LANES), 1)
    o_ref[0] = jnp.where(lane_o < LANES // 2, on[:tq], on[tq:]).astype(o_ref.dtype)


def _moba_prompt(q, k, v, kmean_pad, *, tq, ck):
    b, l, _ = q.shape
    assert l % ck == 0 and ck % tq == 0 and tq % MOBA_BLOCK == 0 and ck % MOBA_BLOCK == 0
    assert l // MOBA_BLOCK <= LANES
    return pl.pallas_call(
        functools.partial(_moba_prompt_kernel, ck=ck),
        name="moba_prompt",
        grid=(b, W_B // LANES, l // tq),
        in_specs=[pl.BlockSpec((1, tq, LANES), lambda bi, h, qi: (bi, qi, h)),
                  pl.BlockSpec((1, l, LANES), lambda bi, h, qi: (bi, 0, h)),
                  pl.BlockSpec((1, l, LANES), lambda bi, h, qi: (bi, 0, h)),
                  pl.BlockSpec((1, LANES, LANES), lambda bi, h, qi: (bi, 0, h))],
        out_specs=pl.BlockSpec((1, tq, LANES), lambda bi, h, qi: (bi, qi, h)),
        out_shape=jax.ShapeDtypeStruct((b, l, W_B), BF16),
        scratch_shapes=[pltpu.VMEM((2 * tq, LANES), F32), pltpu.VMEM((2 * tq, 2 * LANES), F32),
                        pltpu.VMEM((2 * tq, ck), F32), pltpu.VMEM((2 * tq, ck), F32)],
        compiler_params=_cparams(("parallel", "parallel", "arbitrary")),
    )(q, k, v, kmean_pad)


def _layer_norm(z, g, b):
    mu = jnp.mean(z, axis=1, keepdims=True)
    zc = z - mu
    var = jnp.mean(zc * zc, axis=1, keepdims=True)
    return zc * lax.rsqrt(var + LN_EPS) * g + b


def _first_argmax(vals, lane_f, valid):
    ninf = jnp.float32(-jnp.inf)
    v = jnp.where(valid, vals, ninf)
    mx = jnp.max(v, axis=1, keepdims=True)
    idx = jnp.min(jnp.where(v == mx, lane_f, float(LANES)), axis=1, keepdims=True)
    return mx, idx


def _route(logits):
    lane_i = lax.broadcasted_iota(jnp.int32, logits.shape, 1)
    lane = lane_i.astype(F32)
    is_grp = lane_i < N_GROUPS
    gmax, gidx = _first_argmax(logits, lane, is_grp)
    gex = jnp.where(is_grp, jnp.exp(logits - gmax), 0.0)
    gp = gex / jnp.sum(gex, axis=1, keepdims=True)
    g_p, g_idx = _first_argmax(gp, lane, is_grp)
    lo = N_GROUPS + g_idx * EXPERTS_PER_GROUP
    in_grp = (lane >= lo) & (lane < lo + EXPERTS_PER_GROUP)
    emax, _ = _first_argmax(logits, lane, in_grp)
    eex = jnp.where(in_grp, jnp.exp(logits - emax), 0.0)
    ep = eex / jnp.sum(eex, axis=1, keepdims=True)
    p1, i1 = _first_argmax(ep, lane, in_grp)
    p2, i2 = _first_argmax(ep, lane, in_grp & (lane != i1))
    den = p1 + p2
    e1 = g_idx * EXPERTS_PER_GROUP + (i1 - lo)
    e2 = g_idx * EXPERTS_PER_GROUP + (i2 - lo)
    out = jnp.where(lane_i == 0, e1, 0.0)
    out = jnp.where(lane_i == 1, e2, out)
    out = jnp.where(lane_i == 2, g_p * (p1 / den), out)
    out = jnp.where(lane_i == 3, g_p * (p2 / den), out)
    return out


def _merge_kernel(x_ref, oa_ref, ob_ref, ga_ref, gb_ref, wa_ref, wb_ref, wo_ref, g1_ref, b1_ref,
                  wr_ref, br_ref, h_ref, route_ref, *, alpha):
    br_a = jnp.dot(oa_ref[...], wa_ref[...], preferred_element_type=F32)
    br_b = jnp.dot(ob_ref[...], wb_ref[...], preferred_element_type=F32)
    t = ga_ref[...] * br_a + gb_ref[...] * br_b
    mix = jnp.dot(t.astype(BF16), wo_ref[...], preferred_element_type=F32)
    h = _layer_norm(alpha * x_ref[...] + mix, g1_ref[...], b1_ref[...])
    h_ref[...] = h
    logits = jnp.dot(h.astype(BF16), wr_ref[...], preferred_element_type=F32) + br_ref[...]
    route_ref[...] = _route(logits)


def _merge(x2d, oa, ob, ga, gb, wa16, wb16, wo16, g1, b1, wr16, br, alpha, *, tm):
    n, d = x2d.shape
    assert n % tm == 0
    row = lambda i: (i, 0)
    fix = lambda i: (0, 0)
    return pl.pallas_call(
        functools.partial(_merge_kernel, alpha=alpha),
        grid=(n // tm,),
        in_specs=[pl.BlockSpec((tm, d), row), pl.BlockSpec((tm, W_A), row), pl.BlockSpec((tm, W_B), row),
                  pl.BlockSpec((tm, d), row), pl.BlockSpec((tm, d), row),
                  pl.BlockSpec((W_A, d), fix), pl.BlockSpec((W_B, d), fix), pl.BlockSpec((d, d), fix),
                  pl.BlockSpec((1, d), fix), pl.BlockSpec((1, d), fix),
                  pl.BlockSpec((d, LANES), fix), pl.BlockSpec((1, LANES), fix)],
        out_specs=[pl.BlockSpec((tm, d), row), pl.BlockSpec((tm, LANES), row)],
        out_shape=[jax.ShapeDtypeStruct((n, d), F32), jax.ShapeDtypeStruct((n, LANES), F32)],
        compiler_params=_cparams(("parallel",)),
    )(x2d, oa, ob, ga, gb, wa16, wb16, wo16, g1, b1, wr16, br)


GATHER_UNROLL = 8


def _row_copy(idx_ref, r, src_hbm, dst_vmem, sem):
    return pltpu.make_async_copy(src_hbm.at[pl.ds(idx_ref[r], 1), :], dst_vmem.at[pl.ds(r, 1), :], sem)


def _gather_start(idx_ref, src_hbm, dst_vmem, sem):
    def body(g, c):
        for u in range(GATHER_UNROLL):
            _row_copy(idx_ref, g * GATHER_UNROLL + u, src_hbm, dst_vmem, sem).start()
        return c
    lax.fori_loop(0, dst_vmem.shape[0] // GATHER_UNROLL, body, 0)


def _gather_wait(idx_ref, src_hbm, dst_vmem, sem):
    def body(g, c):
        for u in range(GATHER_UNROLL):
            _row_copy(idx_ref, g * GATHER_UNROLL + u, src_hbm, dst_vmem, sem).wait()
        return c
    lax.fori_loop(0, dst_vmem.shape[0] // GATHER_UNROLL, body, 0)


def _expert_kernel(tile_e_ref, n_used_ref, src_ref, nxt_ref, h_hbm, wg_ref, wu_ref, wd_ref, out_ref, xbuf, sem):
    i = pl.program_id(0)
    n_used = n_used_ref[0]
    slot = i % 2

    @pl.when((i == 0) & (i < n_used))
    def _():
        _gather_start(src_ref, h_hbm, xbuf.at[0], sem.at[0])

    @pl.when(i + 1 < n_used)
    def _():
        _gather_start(nxt_ref, h_hbm, xbuf.at[1 - slot], sem.at[1 - slot])

    @pl.when(i < n_used)
    def _():
        _gather_wait(src_ref, h_hbm, xbuf.at[slot], sem.at[slot])
        xb = xbuf[slot].astype(BF16)
        a = jnp.dot(xb, wg_ref[0], preferred_element_type=F32)
        b = jnp.dot(xb, wu_ref[0], preferred_element_type=F32)
        act = (a * jax.nn.sigmoid(a)) * b
        out_ref[...] = jnp.dot(act.astype(BF16), wd_ref[0], preferred_element_type=F32)

    @pl.when(i >= n_used)
    def _():
        out_ref[...] = jnp.zeros(out_ref.shape, F32)


def _experts(h, src_tok, tile_expert, n_used, wg16, wu16, wd16, *, tm):
    n, d = h.shape
    r_max = src_tok.shape[0]
    de = wg16.shape[2]
    n_tiles = r_max // tm
    assert tm % GATHER_UNROLL == 0
    grid_spec = pltpu.PrefetchScalarGridSpec(
        num_scalar_prefetch=2,
        grid=(n_tiles,),
        in_specs=[pl.BlockSpec((tm,), lambda i, te, nu: (i,), memory_space=pltpu.SMEM),
                  pl.BlockSpec((tm,), lambda i, te, nu: (jnp.minimum(i + 1, n_tiles - 1),), memory_space=pltpu.SMEM),
                  pl.BlockSpec(memory_space=pl.ANY),
                  pl.BlockSpec((1, d, de), lambda i, te, nu: (te[i], 0, 0)),
                  pl.BlockSpec((1, d, de), lambda i, te, nu: (te[i], 0, 0)),
                  pl.BlockSpec((1, de, d), lambda i, te, nu: (te[i], 0, 0))],
        out_specs=pl.BlockSpec((tm, d), lambda i, te, nu: (i, 0)),
        scratch_shapes=[pltpu.VMEM((2, tm, d), F32), pltpu.SemaphoreType.DMA((2,))],
    )
    return pl.pallas_call(
        _expert_kernel,
        name="experts",
        grid_spec=grid_spec,
        out_shape=jax.ShapeDtypeStruct((r_max, d), F32),
        compiler_params=_cparams(("arbitrary",)),
    )(tile_expert, n_used, src_tok, src_tok, h, wg16, wu16, wd16)


def _combine_kernel(pos0_ref, pos1_ref, nxt0_ref, nxt1_ref, h_ref, w_ref, g2_ref, b2_ref, out_hbm, y_ref,
                    buf0, buf1, sem0, sem1, *, alpha):
    i = pl.program_id(0)
    slot = i % 2

    @pl.when(i == 0)
    def _():
        _gather_start(pos0_ref, out_hbm, buf0.at[0], sem0.at[0])
        _gather_start(pos1_ref, out_hbm, buf1.at[0], sem1.at[0])

    @pl.when(i + 1 < pl.num_programs(0))
    def _():
        _gather_start(nxt0_ref, out_hbm, buf0.at[1 - slot], sem0.at[1 - slot])
        _gather_start(nxt1_ref, out_hbm, buf1.at[1 - slot], sem1.at[1 - slot])

    _gather_wait(pos0_ref, out_hbm, buf0.at[slot], sem0.at[slot])
    _gather_wait(pos1_ref, out_hbm, buf1.at[slot], sem1.at[slot])
    w = w_ref[...]
    f = buf0[slot] * w[:, 2:3] + buf1[slot] * w[:, 3:4]
    y_ref[...] = _layer_norm(alpha * h_ref[...] + f, g2_ref[...], b2_ref[...])


def _combine(h, route, pos0, pos1, out_sorted, g2, b2, alpha, *, tm):
    n, d = h.shape
    assert n % tm == 0 and tm % GATHER_UNROLL == 0
    n_tiles = n // tm
    row = lambda i: (i, 0)
    fix = lambda i: (0, 0)
    cur = pl.BlockSpec((tm,), lambda i: (i,), memory_space=pltpu.SMEM)
    nxt = pl.BlockSpec((tm,), lambda i: (jnp.minimum(i + 1, n_tiles - 1),), memory_space=pltpu.SMEM)
    return pl.pallas_call(
        functools.partial(_combine_kernel, alpha=alpha),
        name="combine",
        grid=(n_tiles,),
        in_specs=[cur, cur, nxt, nxt,
                  pl.BlockSpec((tm, d), row), pl.BlockSpec((tm, LANES), row),
                  pl.BlockSpec((1, d), fix), pl.BlockSpec((1, d), fix),
                  pl.BlockSpec(memory_space=pl.ANY)],
        out_specs=pl.BlockSpec((tm, d), row),
        out_shape=jax.ShapeDtypeStruct((n, d), F32),
        scratch_shapes=[pltpu.VMEM((2, tm, d), F32), pltpu.VMEM((2, tm, d), F32),
                        pltpu.SemaphoreType.DMA((2,)), pltpu.SemaphoreType.DMA((2,))],
        compiler_params=_cparams(("arbitrary",)),
    )(pos0, pos1, pos0, pos1, h, route, g2, b2, out_sorted)


def _moe_plan(route, tm):
    n = route.shape[0]
    flat_e = route[:, :TOPK_IN_GROUP].astype(jnp.int32).reshape(-1)
    n_asg = flat_e.shape[0]
    order = jnp.argsort(flat_e, stable=True).astype(jnp.int32)
    rank = jnp.argsort(order).astype(jnp.int32)
    experts = jnp.arange(N_EXPERTS, dtype=jnp.int32)
    sizes = jnp.sum((flat_e[:, None] == experts[None, :]).astype(jnp.int32), axis=0)
    padded = ((sizes + tm - 1) // tm) * tm
    pend = jnp.cumsum(padded)
    pstart = pend - padded
    start = jnp.cumsum(sizes) - sizes
    pos = (pstart[flat_e] + (rank - start[flat_e])).reshape(n, TOPK_IN_GROUP)
    r_max = ((n_asg + N_EXPERTS * (tm - 1)) // tm + 1) * tm
    tile_start = jnp.arange(r_max // tm, dtype=jnp.int32) * tm
    tile_expert = jnp.minimum(jnp.sum((tile_start[:, None] >= pend[None, :]).astype(jnp.int32), axis=1),
                              N_EXPERTS - 1)
    n_used = (pend[-1] // tm).astype(jnp.int32).reshape(1)
    row = jnp.arange(r_max, dtype=jnp.int32)
    row_e = jnp.repeat(tile_expert, tm)
    off = row - pstart[row_e]
    slot = jnp.clip(start[row_e] + off, 0, n_asg - 1)
    src_tok = jnp.where(off < sizes[row_e], order[slot] // TOPK_IN_GROUP, 0).astype(jnp.int32)
    return src_tok, tile_expert, n_used, pos[:, 0], pos[:, 1]


ROWS_PER_HEAD = SUBLANES
N_SUB = 8
QROWS = N_SUB * ROWS_PER_HEAD


def _block_diag_queries(q):
    s, t, w = q.shape
    qp = jnp.pad(q, ((0, 0), (0, ROWS_PER_HEAD - t), (0, 0)))
    tiled = jnp.tile(qp, (1, N_SUB, 1))
    row_head = jnp.arange(QROWS)[:, None] // ROWS_PER_HEAD
    lane_head = jnp.arange(w)[None, :] // (w // N_SUB)
    return jnp.where((row_head == lane_head)[None], tiled, jnp.zeros_like(tiled))


def _new_token_mask(n_new):
    row = lax.broadcasted_iota(jnp.int32, (QROWS, LANES), 0) % ROWS_PER_HEAD
    col = lax.broadcasted_iota(jnp.int32, (QROWS, LANES), 1)
    return (col <= row) & (col < n_new)


def _diff_sample_kernel(pt_ref, lam_ref, sub_ref, q_ref, kn_ref, vn_ref, *refs, pages, n_new, lambda_init):
    k_refs = refs[:pages]
    v_refs = refs[pages:2 * pages]
    o_ref, m_ref, l_ref, acc_ref = refs[2 * pages:]
    c = pl.program_id(1)
    q = q_ref[0]
    page = k_refs[0].shape[3]
    flat = H_A * page
    e_row = lax.broadcasted_iota(jnp.int32, (page, flat), 0)
    e_col = lax.broadcasted_iota(jnp.int32, (page, flat), 1)
    spread = jnp.where(e_col // H_A == e_row, 1.0, 0.0).astype(BF16)
    r_head = lax.broadcasted_iota(jnp.int32, (QROWS, flat), 0) // (2 * ROWS_PER_HEAD)
    r_col = lax.broadcasted_iota(jnp.int32, (QROWS, flat), 1) % H_A
    own_head = r_head == r_col

    def pv(pb, v_flat):
        p4 = jnp.where(own_head, jnp.dot(pb, spread, preferred_element_type=F32), 0.0).astype(BF16)
        return jnp.dot(p4, v_flat, preferred_element_type=F32)

    def update(s, v_pages):
        m_old = m_ref[...]
        m_new = jnp.maximum(m_old, jnp.max(s, axis=1, keepdims=True))
        alpha = jnp.exp(m_old - m_new)
        p = jnp.exp(s - m_new)
        l_ref[...] = alpha * l_ref[...] + jnp.sum(p, axis=1, keepdims=True)
        pb = p.astype(BF16)
        upd = pv(pb[:, :page], v_pages[0])
        for j in range(1, len(v_pages)):
            upd = upd + pv(pb[:, j * page:(j + 1) * page], v_pages[j])
        acc_ref[...] = alpha * acc_ref[...] + upd
        m_ref[...] = m_new

    @pl.when(c == 0)
    def _():
        m_ref[...] = jnp.full(m_ref.shape, NEG, F32)
        l_ref[...] = jnp.zeros(l_ref.shape, F32)
        acc_ref[...] = jnp.zeros(acc_ref.shape, F32)
        s = lax.dot_general(q, kn_ref[0], _NT, preferred_element_type=F32)
        s = jnp.where(_new_token_mask(n_new), s, NEG)
        update(s, [vn_ref[0]])

    kt = jnp.concatenate([k_refs[j][0].reshape(q.shape[1], -1).astype(BF16) for j in range(pages)], axis=1)
    s = jnp.dot(q, kt, preferred_element_type=F32)
    update(s, [v_refs[j][0].reshape(flat, LANES).astype(BF16) for j in range(pages)])

    @pl.when(c == pl.num_programs(1) - 1)
    def _():
        on = acc_ref[...] / l_ref[...]
        lam = _diff_lambda(lam_ref, lambda_init)
        for h in range(H_A):
            o1 = on[(2 * h) * ROWS_PER_HEAD:(2 * h + 1) * ROWS_PER_HEAD]
            o2 = on[(2 * h + 1) * ROWS_PER_HEAD:(2 * h + 2) * ROWS_PER_HEAD]
            o_ref[0, :, h * LANES:(h + 1) * LANES] = _diff_finish(o1, o2, lam, sub_ref[...], lambda_init).astype(o_ref.dtype)


def _page_specs(pages, page_shape, n_chunks, phase):
    def make(j):
        def index_map(b, c, pt):
            if phase == 0:
                c = jnp.minimum(c, n_chunks - 1)
            elif phase == 1:
                c = jnp.maximum(c - n_chunks, 0)
            return (pt[b, c * pages + j],) + (0,) * len(page_shape)
        return pl.BlockSpec((1,) + tuple(page_shape), index_map)
    return [make(j) for j in range(pages)]


def _diff_sample(page_table, diff_lambda, subln, qbd, k_new, v_new, cache_kt, cache_v, lambda_init, *, pages, n_new):
    s, n_pages = page_table.shape
    width = qbd.shape[2]
    page = cache_kt.shape[3]
    assert n_pages % pages == 0
    n_chunks = n_pages // pages
    per_seq = lambda b, c, pt: (b, 0, 0)
    fix = lambda b, c, pt: (0, 0)
    grid_spec = pltpu.PrefetchScalarGridSpec(
        num_scalar_prefetch=1,
        grid=(s, n_chunks),
        in_specs=[pl.BlockSpec((4, HD_A), fix), pl.BlockSpec((1, LANES), fix),
                  pl.BlockSpec((1, QROWS, width), per_seq),
                  pl.BlockSpec((1, LANES, width), per_seq), pl.BlockSpec((1, H_A * LANES, LANES), per_seq)]
        + _page_specs(pages, cache_kt.shape[1:], n_chunks, None) + _page_specs(pages, cache_v.shape[1:], n_chunks, None),
        out_specs=pl.BlockSpec((1, ROWS_PER_HEAD, width), per_seq),
        scratch_shapes=[pltpu.VMEM((QROWS, 1), F32), pltpu.VMEM((QROWS, 1), F32), pltpu.VMEM((QROWS, LANES), F32)],
    )
    assert page == LANES and cache_v.shape[1:] == (page, H_A, LANES)
    v_new_flat = v_new.reshape(s, H_A * LANES, LANES)
    return pl.pallas_call(
        functools.partial(_diff_sample_kernel, pages=pages, n_new=n_new, lambda_init=lambda_init),
        name="diff_sample",
        grid_spec=grid_spec,
        out_shape=jax.ShapeDtypeStruct((s, ROWS_PER_HEAD, width), BF16),
        compiler_params=_cparams(("parallel", "arbitrary")),
    )(page_table, diff_lambda, subln, qbd, k_new, v_new_flat, *([cache_kt] * pages), *([cache_v] * pages))


def _moba_sample_kernel(pt_ref, q_ref, kn_ref, vn_ref, *refs, pages, page, n_chunks, n_new):
    k_refs = refs[:pages]
    v_refs = refs[pages:2 * pages]
    o_ref, s_ref, p_ref, gate_ref, l_ref, acc_ref = refs[2 * pages:]
    c = pl.program_id(1)
    q = q_ref[0]
    span = pages * page
    blocks_per_page = MOBA_BLOCK // page
    lane = lax.broadcasted_iota(jnp.int32, (QROWS, LANES), 1)

    @pl.when(c == 0)
    def _():
        gate_ref[...] = jnp.zeros(gate_ref.shape, F32)

    @pl.when(c < n_chunks)
    def _():
        g = gate_ref[...]
        kt = jnp.concatenate([k_refs[j][0].reshape(q.shape[1], -1).astype(BF16) for j in range(pages)], axis=1)
        s_chunk = jnp.dot(q, kt, preferred_element_type=F32)
        s_ref[:, pl.ds(pl.multiple_of(c * span, span), span)] = s_chunk
        for j in range(pages):
            sj = s_chunk[:, j * page:(j + 1) * page]
            blk = (c * pages + j) // blocks_per_page
            g = g + jnp.where(lane == blk, jnp.sum(sj, axis=1, keepdims=True) * (1.0 / MOBA_BLOCK), 0.0)
        gate_ref[...] = g

    @pl.when(c == n_chunks - 1)
    def _():
        n_past = (n_chunks * span) // MOBA_BLOCK
        bias = _moba_select(gate_ref[...], float(n_past))
        s_new = lax.dot_general(q, kn_ref[0], _NT, preferred_element_type=F32)
        s_new = jnp.where(_new_token_mask(n_new), s_new, NEG)
        m = jnp.max(s_new, axis=1, keepdims=True)
        bias16 = bias.astype(BF16)
        brow = lax.broadcasted_iota(jnp.int32, (LANES, span), 0)
        bcol = lax.broadcasted_iota(jnp.int32, (LANES, span), 1)
        for cc in range(n_chunks):
            expand = jnp.where((bcol + cc * span) // MOBA_BLOCK == brow, 1.0, 0.0).astype(BF16)
            sc = s_ref[:, cc * span:(cc + 1) * span] + jnp.dot(bias16, expand, preferred_element_type=F32)
            s_ref[:, cc * span:(cc + 1) * span] = sc
            m = jnp.maximum(m, jnp.max(sc, axis=1, keepdims=True))
        p_new = jnp.exp(s_new - m)
        l = jnp.sum(p_new, axis=1, keepdims=True)
        for cc in range(n_chunks):
            pc = jnp.exp(s_ref[:, cc * span:(cc + 1) * span] - m)
            l = l + jnp.sum(pc, axis=1, keepdims=True)
            p_ref[:, cc * span:(cc + 1) * span] = pc.astype(BF16)
        l_ref[...] = l
        acc_ref[...] = jnp.dot(p_new.astype(BF16), vn_ref[0], preferred_element_type=F32)

    @pl.when(c >= n_chunks)
    def _():
        cv = c - n_chunks
        acc = acc_ref[...]
        for j in range(pages):
            pj = p_ref[:, pl.ds(pl.multiple_of(cv * span + j * page, page), page)]
            vt = v_refs[j][0].reshape(acc.shape[1], -1).astype(BF16)
            acc = acc + lax.dot_general(pj, vt, _NT, preferred_element_type=F32)
        acc_ref[...] = acc

    @pl.when(c == 2 * n_chunks - 1)
    def _():
        on = acc_ref[...] / l_ref[...]
        lane_o = lax.broadcasted_iota(jnp.int32, (ROWS_PER_HEAD, on.shape[1]), 1) // HD_B
        out = jnp.zeros((ROWS_PER_HEAD, on.shape[1]), F32)
        for h in range(H_B):
            out = jnp.where(lane_o == h, on[h * ROWS_PER_HEAD:(h + 1) * ROWS_PER_HEAD, :], out)
        o_ref[0] = out.astype(o_ref.dtype)


def _moba_sample(page_table, qbd, k_new, v_new, cache_kt, cache_vt, *, pages, n_new):
    s, n_pages = page_table.shape
    width = qbd.shape[2]
    page = cache_kt.shape[3]
    assert n_pages % pages == 0 and MOBA_BLOCK % page == 0 and (n_pages * page) % MOBA_BLOCK == 0
    assert (n_pages * page) // MOBA_BLOCK <= LANES
    n_chunks = n_pages // pages
    past = n_pages * page
    per_seq = lambda b, c, pt: (b, 0, 0)
    grid_spec = pltpu.PrefetchScalarGridSpec(
        num_scalar_prefetch=1,
        grid=(s, 2 * n_chunks),
        in_specs=[pl.BlockSpec((1, QROWS, width), per_seq),
                  pl.BlockSpec((1, LANES, width), per_seq), pl.BlockSpec((1, LANES, width), per_seq)]
        + _page_specs(pages, cache_kt.shape[1:], n_chunks, 0) + _page_specs(pages, cache_vt.shape[1:], n_chunks, 1),
        out_specs=pl.BlockSpec((1, ROWS_PER_HEAD, width), per_seq),
        scratch_shapes=[pltpu.VMEM((QROWS, past), F32), pltpu.VMEM((QROWS, past), BF16),
                        pltpu.VMEM((QROWS, LANES), F32), pltpu.VMEM((QROWS, 1), F32),
                        pltpu.VMEM((QROWS, width), F32)],
    )
    return pl.pallas_call(
        functools.partial(_moba_sample_kernel, pages=pages, page=page, n_chunks=n_chunks, n_new=n_new),
        name="moba_sample",
        grid_spec=grid_spec,
        out_shape=jax.ShapeDtypeStruct((s, ROWS_PER_HEAD, width), BF16),
        compiler_params=_cparams(("parallel", "arbitrary")),
    )(page_table, qbd, k_new, v_new, *([cache_kt] * pages), *([cache_vt] * pages))


def _pad_rows(a, rows):
    return jnp.pad(a, ((0, 0), (0, rows - a.shape[1]), (0, 0)))


def _ffn(x2d, oa, ob, ga, gb, wts, alpha, *, tm_merge, tm_exp, tm_comb):
    h, route = _merge(x2d, oa, ob, ga, gb, wts['wa'], wts['wb'], wts['wo'], wts['g1'], wts['b1'],
                      wts['wr'], wts['br'], alpha, tm=tm_merge)
    src_tok, tile_expert, n_used, pos0, pos1 = _moe_plan(route, tm_exp)
    out_sorted = _experts(h, src_tok, tile_expert, n_used, wts['wg'], wts['wu'], wts['wd'], tm=tm_exp)
    return _combine(h, route, pos0, pos1, out_sorted, wts['g2'], wts['b2'], alpha, tm=tm_comb)


def kernel(x_prompt, x_sample, cache_k_diff, cache_v_diff, cache_k_moba, cache_v_moba, page_table,
           w_in, w_br_a, w_br_b, w_out, diff_lambda, diff_subln, ln1_g, ln1_b,
           w_grp, b_grp, w_exp, b_exp, w_gate, w_up, w_down, ln2_g, ln2_b):
    depth = w_in.shape[0]
    assert depth == 1, "kernel written for a single layer"
    b, l, d = x_prompt.shape
    s, t, _ = x_sample.shape
    n_pool, page = cache_k_diff.shape[1], cache_k_diff.shape[2]
    past_len = page_table.shape[1] * page
    alpha = (2.0 * depth) ** 0.25
    lyr = 0
    lambda_init = 0.8 - 0.6 * math.exp(-0.3 * lyr)

    w_in16 = w_in[lyr].astype(BF16)
    n_route = N_GROUPS + N_EXPERTS
    wr = jnp.concatenate([w_grp[lyr], w_exp[lyr], jnp.zeros((d, LANES - n_route), F32)], axis=1).astype(BF16)
    br = jnp.concatenate([b_grp[lyr], b_exp[lyr], jnp.zeros((LANES - n_route,), F32)]).reshape(1, LANES)
    wts = dict(wa=w_br_a[lyr].astype(BF16), wb=w_br_b[lyr].astype(BF16), wo=w_out[lyr].astype(BF16),
               g1=ln1_g[lyr].reshape(1, d), b1=ln1_b[lyr].reshape(1, d), wr=wr, br=br,
               wg=w_gate[lyr].astype(BF16), wu=w_up[lyr].astype(BF16), wd=w_down[lyr].astype(BF16),
               g2=ln2_g[lyr].reshape(1, d), b2=ln2_b[lyr].reshape(1, d))
    lam_p = diff_lambda[lyr]
    subln = diff_subln[lyr].reshape(1, 2 * HD_A)

    cos, slo, shi, half = _rope_tables(jnp.arange(l, dtype=jnp.int32), HD_A)
    tm = min(512, l)
    xp2 = x_prompt.reshape(b * l, d)
    (kd32, vd32, km32, vm32, qd, kd, vd, qm, km, vm, ga, gb, kmean) = _proj(
        xp2, w_in16, cos, slo, shi, half, tm=tm, with_means=True)
    r3 = lambda a: a.reshape(b, l, a.shape[-1])
    ck = min(KEY_CHUNK, l)
    tq = min(Q_TILE, ck)
    oa = _diff_prompt(r3(qd), r3(kd), r3(vd), lam_p, subln, lambda_init, tq=tq, ck=ck)
    nb = l // MOBA_BLOCK
    kmean_pad = jnp.pad(kmean.reshape(b, nb, W_B), ((0, 0), (0, LANES - nb), (0, 0)))
    ob = _moba_prompt(r3(qm), r3(km), r3(vm), kmean_pad, tq=tq, ck=ck)
    y_p = _ffn(xp2, oa.reshape(b * l, W_A), ob.reshape(b * l, W_B), ga, gb, wts, alpha,
               tm_merge=tm, tm_exp=256, tm_comb=256).reshape(b, l, d)
    kd_p = kd32.reshape(1, b, l, 2 * H_A, HD_A)
    vd_p = vd32.reshape(1, b, l, H_A, 2 * HD_A)
    km_p = km32.reshape(1, b, l, H_B, HD_B)
    vm_p = vm32.reshape(1, b, l, H_B, HD_B)

    pos_s = past_len + (jnp.arange(s * t, dtype=jnp.int32) % t)
    cos, slo, shi, half = _rope_tables(pos_s, HD_A)
    xs2 = x_sample.reshape(s * t, d)
    (kd32, vd32, km32, vm32, qd, kd, vd, qm, km, vm, ga, gb) = _proj(
        xs2, w_in16, cos, slo, shi, half, tm=s * t, with_means=False)
    r3 = lambda a: a.reshape(s, t, a.shape[-1])
    pages = math.gcd(SAMPLE_PAGES_PER_STEP, page_table.shape[1])
    tr = lambda cch: jnp.transpose(cch[lyr], (0, 2, 3, 1))
    oa = _diff_sample(page_table, lam_p, subln, _block_diag_queries(r3(qd)), _pad_rows(r3(kd), LANES),
                      _pad_rows(r3(vd), LANES), tr(cache_k_diff), cache_v_diff[lyr], lambda_init, pages=pages, n_new=t)
    ob = _moba_sample(page_table, _block_diag_queries(r3(qm)), _pad_rows(r3(km), LANES), _pad_rows(r3(vm), LANES),
                      tr(cache_k_moba), tr(cache_v_moba), pages=pages, n_new=t)
    oa = oa[:, :t].reshape(s * t, W_A)
    ob = ob[:, :t].reshape(s * t, W_B)
    y_s = _ffn(xs2, oa, ob, ga, gb, wts, alpha, tm_merge=s * t, tm_exp=128,
               tm_comb=min(256, s * t)).reshape(s, t, d)
    kd_s = kd32.reshape(1, s, t, 2 * H_A, HD_A)
    vd_s = vd32.reshape(1, s, t, H_A, 2 * HD_A)
    km_s = km32.reshape(1, s, t, H_B, HD_B)
    vm_s = vm32.reshape(1, s, t, H_B, HD_B)

    return (y_p, y_s, kd_p, vd_p, km_p, vm_p, kd_s, vd_s, km_s, vm_s)
```
